```python
import math
import jax
import jax.numpy as jnp
from jax import lax
import numpy as np

D_MODEL = 1024
BATCH = 16
SEQ = 2048
DEPTH = 2
DEC_BATCH = 32
DEC_SEQ = 8
PAST_LEN = 16384
PAGE_SIZE = 128

N_A_LAYERS = DEPTH // 2
N_B_LAYERS = DEPTH - N_A_LAYERS

SSM_EXPAND = 2
D_INNER = SSM_EXPAND * D_MODEL
SSM_HEAD_DIM = 64
SSM_HEADS = D_INNER // SSM_HEAD_DIM
SSM_GROUPS = 8
SSM_HEADS_PER_GROUP = SSM_HEADS // SSM_GROUPS
D_STATE = 128
CONV_W = 4
CONV_DIM = D_INNER + 2 * SSM_GROUPS * D_STATE
IN_PROJ_DIM = D_INNER + CONV_DIM + SSM_HEADS
SSD_CHUNK = 128

HEAD_DIM = 64
DIL_PATTERNS = ((128, 1), (512, 4), (2048, 16))
N_DIL = len(DIL_PATTERNS)
HEADS_PER_BRANCH = D_MODEL // (2 * HEAD_DIM)
KV_PER_BRANCH = 2
Q_PER_KV = HEADS_PER_BRANCH // KV_PER_BRANCH
N_Q_HEADS = N_DIL * HEADS_PER_BRANCH
N_KV_HEADS = N_DIL * KV_PER_BRANCH
ATTN_OUT_DIM = HEADS_PER_BRANCH * HEAD_DIM
WIN_MAX = max(w for w, _ in DIL_PATTERNS)
WIN_BUF = min(WIN_MAX, PAST_LEN)
ROPE_DIM = HEAD_DIM // 4
ROPE_THETA = 500000.0
Q_BLOCK = 128

PEER_HEADS = 8
PEER_N_KEYS = 128
PEER_N_EXPERTS = PEER_N_KEYS * PEER_N_KEYS
PEER_TOPK = 16
PEER_QUERY_DIM = 256
PEER_HALF = PEER_QUERY_DIM // 2
PEER_TOKEN_BLOCK = 256

PLE_DIM = 256
EPS = 1e-6

kernel_name = 'yoco_ssd_dilated_peer_step'


def rmsnorm(x, g):
    xf = x.astype(jnp.float32)
    y = xf * lax.rsqrt(jnp.mean(xf * xf, axis=-1, keepdims=True) + EPS)
    return (y * g.astype(jnp.float32)).astype(x.dtype)


def rope_partial(x, pos):
    half = ROPE_DIM // 2
    inv = ROPE_THETA ** (-jnp.arange(0, ROPE_DIM, 2, dtype=jnp.float32) / ROPE_DIM)
    ang = pos.astype(jnp.float32)[:, None] * inv[None, :]
    cos = jnp.cos(ang)[None, :, None, :]
    sin = jnp.sin(ang)[None, :, None, :]
    xr = x[..., :ROPE_DIM].astype(jnp.float32)
    x1, x2 = xr[..., :half], xr[..., half:]
    rot = jnp.concatenate([x1 * cos - x2 * sin, x2 * cos + x1 * sin], axis=-1)
    return jnp.concatenate([rot.astype(x.dtype), x[..., ROPE_DIM:]], axis=-1)


def causal_conv(xbc, buf, w, b):
    xp = jnp.concatenate([buf.astype(xbc.dtype), xbc], axis=1)
    y = lax.conv_general_dilated(xp, w[:, None, :].astype(xp.dtype), window_strides=(1,), padding='VALID',
                                 dimension_numbers=('NWC', 'WIO', 'NWC'), feature_group_count=CONV_DIM)
    return y + b.astype(y.dtype), xp[:, -(CONV_W - 1):]


def ssd_scan(xs, dt, a, bm, cm, h0):
    f32 = jnp.float32
    b_, t_ = xs.shape[:2]
    L = SSD_CHUNK if t_ % SSD_CHUNK == 0 else t_
    nc = t_ // L
    G, HG, P, N = SSM_GROUPS, SSM_HEADS_PER_GROUP, SSM_HEAD_DIM, D_STATE

    def chunks(z):
        return jnp.swapaxes(z.reshape((b_, nc, L) + z.shape[2:]), 0, 1)

    xdt = chunks((xs.astype(f32) * dt[..., None]).reshape(b_, t_, G, HG, P))
    da = chunks((dt * a).reshape(b_, t_, G, HG))
    bc = chunks(bm.astype(f32))
    cc = chunks(cm.astype(f32))
    causal = jnp.tril(jnp.ones((L, L), dtype=bool))[None, :, :, None, None]

    def step(h, inp):
        xdt_c, da_c, b_c, c_c = inp
        cum = jnp.cumsum(da_c, axis=1)
        seg = cum[:, :, None] - cum[:, None, :]
        decay = jnp.exp(jnp.where(causal, seg, -jnp.inf))
        cb = jnp.einsum('btgn,bsgn->btsg', c_c, b_c)
        y = jnp.einsum('btsgh,bsghp->btghp', cb[..., None] * decay, xdt_c)
        y = y + jnp.einsum('btgn,bghpn->btghp', c_c, h) * jnp.exp(cum)[..., None]
        tail = jnp.exp(cum[:, -1:] - cum)
        h = h * jnp.exp(cum[:, -1])[..., None, None] + jnp.einsum('bsgn,bsgh,bsghp->bghpn', b_c, tail, xdt_c)
        return h, y

    h_fin, ys = lax.scan(step, h0.astype(f32).reshape(b_, G, HG, P, N), (xdt, da, bc, cc))
    y = jnp.swapaxes(ys, 0, 1).reshape(b_, t_, SSM_HEADS, P)
    return y, h_fin.reshape(b_, SSM_HEADS, P, N)


def mamba_mixer(u, conv_buf, h0, w_in, conv_w, conv_b, dt_bias, a_log, d_skip, norm_g, w_out):
    f32 = jnp.float32
    b_, t_, _ = u.shape
    zxbcdt = u @ w_in
    z = zxbcdt[..., :D_INNER]
    xbc = zxbcdt[..., D_INNER:D_INNER + CONV_DIM]
    dt = zxbcdt[..., D_INNER + CONV_DIM:]
    xbc, new_buf = causal_conv(xbc, conv_buf, conv_w, conv_b)
    xbc = jax.nn.silu(xbc)
    gn = SSM_GROUPS * D_STATE
    xs = xbc[..., :D_INNER].reshape(b_, t_, SSM_HEADS, SSM_HEAD_DIM)
    bm = xbc[..., D_INNER:D_INNER + gn].reshape(b_, t_, SSM_GROUPS, D_STATE)
    cm = xbc[..., D_INNER + gn:].reshape(b_, t_, SSM_GROUPS, D_STATE)
    dt = jax.nn.softplus(dt.astype(f32) + dt_bias.astype(f32))
    a = -jnp.exp(a_log.astype(f32))
    y, h = ssd_scan(xs, dt, a, bm, cm, h0)
    y = y + d_skip.astype(f32)[:, None] * xs.astype(f32)
    y = y.reshape(b_, t_, D_INNER) * jax.nn.silu(z.astype(f32))
    yg = y.reshape(b_, t_, SSM_GROUPS, D_INNER // SSM_GROUPS)
    yg = yg * lax.rsqrt(jnp.mean(yg * yg, axis=-1, keepdims=True) + EPS)
    y = (yg.reshape(b_, t_, D_INNER) * norm_g.astype(f32)).astype(u.dtype)
    return y @ w_out, new_buf, h.astype(h0.dtype)


def peer(u, w_query, sub_keys, expert_u, expert_v):
    f32 = jnp.float32
    b_, t_, d_ = u.shape
    n = b_ * t_
    blk = PEER_TOKEN_BLOCK
    n_pad = -(-n // blk) * blk
    flat = jnp.pad(u.reshape(n, d_), ((0, n_pad - n), (0, 0)))
    kk = PEER_TOPK

    def block(xb):
        q = (xb @ w_query).reshape(blk, PEER_HEADS, 2, PEER_HALF).astype(f32)
        s = jnp.einsum('thcd,ckd->thck', q, sub_keys.astype(f32))
        top_s, top_i = lax.top_k(s, kk)
        cand = top_s[:, :, 0, :, None] + top_s[:, :, 1, None, :]
        cand_idx = top_i[:, :, 0, :, None] * PEER_N_KEYS + top_i[:, :, 1, None, :]
        best_s, best_j = lax.top_k(cand.reshape(blk, PEER_HEADS, kk * kk), kk)
        idx = jnp.take_along_axis(cand_idx.reshape(blk, PEER_HEADS, kk * kk), best_j, axis=-1)
        gate = jax.nn.softmax(best_s, axis=-1)
        ue = expert_u[idx]
        ve = expert_v[idx]
        act = jax.nn.gelu(jnp.einsum('td,thkd->thk', xb, ue).astype(f32))
        coef = (gate * act).astype(xb.dtype)
        return jnp.einsum('thk,thkd->td', coef, ve)

    out = lax.map(block, flat.reshape(n_pad // blk, blk, d_))
    return out.reshape(n_pad, d_)[:n].reshape(b_, t_, d_)


def dilated_attention(q, k, v, q_pos0, k_pos0):
    f32 = jnp.float32
    b_, t_ = q.shape[:2]
    s_len = k.shape[1]
    qb = Q_BLOCK if t_ % Q_BLOCK == 0 else t_
    nb = t_ // qb
    q_blocks = jnp.swapaxes(q.reshape(b_, nb, qb, N_DIL, KV_PER_BRANCH, Q_PER_KV, HEAD_DIM), 0, 1)
    starts = q_pos0 + jnp.arange(nb, dtype=jnp.int32) * qb
    scale = HEAD_DIM ** -0.5
    k_br = [k[:, :, g * KV_PER_BRANCH:(g + 1) * KV_PER_BRANCH] for g in range(N_DIL)]
    v_br = [v[:, :, g * KV_PER_BRANCH:(g + 1) * KV_PER_BRANCH] for g in range(N_DIL)]

    def block(args):
        qblk, p0 = args
        qpos = p0 + jnp.arange(qb, dtype=jnp.int32)
        outs, lses = [], []
        for g, (win, dil) in enumerate(DIL_PATTERNS):
            n_keys = win // dil + 1
            kpos = qpos[:, None] - dil * jnp.arange(n_keys, dtype=jnp.int32)[None, :]
            valid = kpos >= 0
            kidx = jnp.clip(kpos - k_pos0, 0, s_len - 1)
            kg = k_br[g][:, kidx].astype(f32)
            vg = v_br[g][:, kidx].astype(f32)
            s = jnp.einsum('bqkrd,bqnkd->bqkrn', qblk[:, :, g].astype(f32), kg) * scale
            s = jnp.where(valid[None, :, None, None, :], s, -jnp.inf)
            lse = jax.nn.logsumexp(s, axis=-1)
            p = jnp.exp(s - lse[..., None])
            outs.append(jnp.einsum('bqkrn,bqnkd->bqkrd', p, vg))
            lses.append(lse)
        wts = jax.nn.softmax(jnp.stack(lses), axis=0)
        return jnp.sum(wts[..., None] * jnp.stack(outs), axis=0)

    o = lax.map(block, (q_blocks, starts))
    return jnp.swapaxes(o, 0, 1).reshape(b_, t_, ATTN_OUT_DIM).astype(q.dtype)


def shared_kv(h, pos, kv_norm, w_kv, k_norm):
    b_, t_, _ = h.shape
    kv = (rmsnorm(h, kv_norm) @ w_kv).reshape(b_, t_, 2, N_KV_HEADS, HEAD_DIM)
    k = rope_partial(rmsnorm(kv[:, :, 0], k_norm), pos)
    return k, kv[:, :, 1]


def trunk(x, p, conv_state, ssm_state, past_k, past_v, pos0,
          norm_mix, norm_ffn, norm_ple,
          ssm_w_in, ssm_conv_w, ssm_conv_b, ssm_dt_bias, ssm_a_log, ssm_d, ssm_norm, ssm_w_out,
          kv_norm, w_kv, k_norm, w_q, q_norm, w_o,
          peer_w_query, peer_sub_keys, peer_u, peer_v, ple_w_proj, ple_w_gate):
    b_, t_, _ = x.shape
    pos = pos0 + jnp.arange(t_, dtype=jnp.int32)
    h = x
    new_conv, new_ssm = [], []
    k_new = v_new = k_all = v_all = None
    k_pos0 = 0
    for i in range(DEPTH):
        u = rmsnorm(h, norm_mix[i])
        if i < N_A_LAYERS:
            y, cb, st = mamba_mixer(u, conv_state[i], ssm_state[i], ssm_w_in[i], ssm_conv_w[i], ssm_conv_b[i],
                                    ssm_dt_bias[i], ssm_a_log[i], ssm_d[i], ssm_norm[i], ssm_w_out[i])
            new_conv.append(cb)
            new_ssm.append(st)
            h = h + y
        else:
            if k_new is None:
                k_new, v_new = shared_kv(h, pos, kv_norm, w_kv, k_norm)
                if past_k is None:
                    k_all, v_all, k_pos0 = k_new, v_new, pos0
                else:
                    k_all = jnp.concatenate([past_k.astype(k_new.dtype), k_new], axis=1)
                    v_all = jnp.concatenate([past_v.astype(v_new.dtype), v_new], axis=1)
                    k_pos0 = pos0 - past_k.shape[1]
            j = i - N_A_LAYERS
            q = (u @ w_q[j]).reshape(b_, t_, N_Q_HEADS, HEAD_DIM)
            q = rope_partial(rmsnorm(q, q_norm[j]), pos)
            h = h + dilated_attention(q, k_all, v_all, pos0, k_pos0) @ w_o[j]
        h = h + peer(rmsnorm(h, norm_ffn[i]), peer_w_query[i], peer_sub_keys[i], peer_u[i], peer_v[i])
        gate = jax.nn.sigmoid((rmsnorm(h, norm_ple[i]) @ ple_w_gate[i]).astype(jnp.float32))
        h = h + ((p[i] @ ple_w_proj[i]).astype(jnp.float32) * gate).astype(h.dtype)
    return h, jnp.stack(new_conv), jnp.stack(new_ssm), k_new, v_new


def setup_inputs(seed: int = 0) -> dict:
    key = jax.random.key(seed)
    ks = iter(list(jax.random.split(key, 48)))
    f32 = jnp.float32

    def nrm(shape, scale):
        return jax.random.normal(next(ks), shape, f32) * scale

    def gain(shape):
        return 1.0 + nrm(shape, 0.02)

    x_prompt = nrm((BATCH, SEQ, D_MODEL), 1.0)
    x_sample = nrm((DEC_BATCH, DEC_SEQ, D_MODEL), 1.0)
    state_conv = nrm((N_A_LAYERS, DEC_BATCH, CONV_W - 1, CONV_DIM), 1.0)
    state_ssm = nrm((N_A_LAYERS, DEC_BATCH, SSM_HEADS, SSM_HEAD_DIM, D_STATE), 0.1)
    cache_k = nrm((DEC_BATCH, WIN_BUF, N_KV_HEADS, HEAD_DIM), 1.0)
    cache_v = nrm((DEC_BATCH, WIN_BUF, N_KV_HEADS, HEAD_DIM), 1.0)
    p_prompt = nrm((DEPTH, BATCH, SEQ, PLE_DIM), 1.0)
    p_sample = nrm((DEPTH, DEC_BATCH, DEC_SEQ, PLE_DIM), 1.0)
    norm_mix = gain((DEPTH, D_MODEL))
    norm_ffn = gain((DEPTH, D_MODEL))
    norm_ple = gain((DEPTH, D_MODEL))
    ssm_w_in = nrm((N_A_LAYERS, D_MODEL, IN_PROJ_DIM), D_MODEL ** -0.5)
    ssm_conv_w = nrm((N_A_LAYERS, CONV_W, CONV_DIM), CONV_W ** -0.5)
    ssm_conv_b = nrm((N_A_LAYERS, CONV_DIM), 0.02)
    dt0 = jnp.exp(jax.random.uniform(next(ks), (N_A_LAYERS, SSM_HEADS), f32, math.log(1e-3), math.log(1e-1)))
    ssm_dt_bias = dt0 + jnp.log(-jnp.expm1(-dt0))
    ssm_a_log = jnp.log(jax.random.uniform(next(ks), (N_A_LAYERS, SSM_HEADS), f32, 1.0, 16.0))
    ssm_d = gain((N_A_LAYERS, SSM_HEADS))
    ssm_norm = gain((N_A_LAYERS, D_INNER))
    ssm_w_out = nrm((N_A_LAYERS, D_INNER, D_MODEL), D_INNER ** -0.5)
    kv_norm = gain((D_MODEL,))
    w_kv = nrm((D_MODEL, 2 * N_KV_HEADS * HEAD_DIM), D_MODEL ** -0.5)
    k_norm = gain((HEAD_DIM,))
    w_q = nrm((N_B_LAYERS, D_MODEL, N_Q_HEADS * HEAD_DIM), D_MODEL ** -0.5)
    q_norm = gain((N_B_LAYERS, HEAD_DIM))
    w_o = nrm((N_B_LAYERS, ATTN_OUT_DIM, D_MODEL), ATTN_OUT_DIM ** -0.5)
    peer_w_query = nrm((DEPTH, D_MODEL, PEER_HEADS * PEER_QUERY_DIM), D_MODEL ** -0.5)
    peer_sub_keys = nrm((DEPTH, 2, PEER_N_KEYS, PEER_HALF), PEER_HALF ** -0.5)
    peer_u = nrm((DEPTH, PEER_N_EXPERTS, D_MODEL), D_MODEL ** -0.5)
    peer_v = nrm((DEPTH, PEER_N_EXPERTS, D_MODEL), 0.5)
    ple_w_proj = nrm((DEPTH, PLE_DIM, D_MODEL), PLE_DIM ** -0.5)
    ple_w_gate = nrm((DEPTH, D_MODEL, D_MODEL), D_MODEL ** -0.5)
    return {'x_prompt': x_prompt, 'x_sample': x_sample, 'state_conv': state_conv, 'state_ssm': state_ssm,
            'cache_k': cache_k, 'cache_v': cache_v, 'p_prompt': p_prompt, 'p_sample': p_sample,
            'norm_mix': norm_mix, 'norm_ffn': norm_ffn, 'norm_ple': norm_ple,
            'ssm_w_in': ssm_w_in, 'ssm_conv_w': ssm_conv_w, 'ssm_conv_b': ssm_conv_b,
            'ssm_dt_bias': ssm_dt_bias, 'ssm_a_log': ssm_a_log, 'ssm_d': ssm_d, 'ssm_norm': ssm_norm,
            'ssm_w_out': ssm_w_out, 'kv_norm': kv_norm, 'w_kv': w_kv, 'k_norm': k_norm,
            'w_q': w_q, 'q_norm': q_norm, 'w_o': w_o,
            'peer_w_query': peer_w_query, 'peer_sub_keys': peer_sub_keys, 'peer_u': peer_u, 'peer_v': peer_v,
            'ple_w_proj': ple_w_proj, 'ple_w_gate': ple_w_gate}


def reference(x_prompt, x_sample, state_conv, state_ssm, cache_k, cache_v, p_prompt, p_sample,
              norm_mix, norm_ffn, norm_ple,
              ssm_w_in, ssm_conv_w, ssm_conv_b, ssm_dt_bias, ssm_a_log, ssm_d, ssm_norm, ssm_w_out,
              kv_norm, w_kv, k_norm, w_q, q_norm, w_o,
              peer_w_query, peer_sub_keys, peer_u, peer_v, ple_w_proj, ple_w_gate):
    weights = (norm_mix, norm_ffn, norm_ple,
               ssm_w_in, ssm_conv_w, ssm_conv_b, ssm_dt_bias, ssm_a_log, ssm_d, ssm_norm, ssm_w_out,
               kv_norm, w_kv, k_norm, w_q, q_norm, w_o,
               peer_w_query, peer_sub_keys, peer_u, peer_v, ple_w_proj, ple_w_gate)
    b_p = x_prompt.shape[0]
    zero_conv = jnp.zeros((N_A_LAYERS, b_p, CONV_W - 1, CONV_DIM), x_prompt.dtype)
    zero_ssm = jnp.zeros((N_A_LAYERS, b_p, SSM_HEADS, SSM_HEAD_DIM, D_STATE), x_prompt.dtype)
    y_prompt, prompt_conv, prompt_ssm, k_p, v_p = trunk(x_prompt, p_prompt, zero_conv, zero_ssm, None, None, 0, *weights)
    keep = min(WIN_MAX, k_p.shape[1])
    prompt_k = k_p[:, -keep:]
    prompt_v = v_p[:, -keep:]
    y_sample, sample_conv, sample_ssm, sample_k, sample_v = trunk(x_sample, p_sample, state_conv, state_ssm,
                                                                  cache_k, cache_v, PAST_LEN, *weights)
    return (y_prompt, y_sample, prompt_conv, prompt_ssm, prompt_k, prompt_v,
            sample_conv, sample_ssm, sample_k, sample_v)
```

```python
import functools
import math

import jax
import jax.numpy as jnp
from jax import lax
from jax.experimental import pallas as pl
from jax.experimental.pallas import tpu as pltpu

f32 = jnp.float32
bf16 = jnp.bfloat16

D_MODEL = 1024
DEPTH = 2
N_A_LAYERS = 1
D_INNER = 2048
SSM_HEAD_DIM = 64
SSM_HEADS = 32
SSM_GROUPS = 8
SSM_HEADS_PER_GROUP = 4
D_STATE = 128
CONV_W = 4
CONV_DIM = 4096
SSD_CHUNK = 128
HEAD_DIM = 64
DIL_PATTERNS = ((128, 1), (512, 4), (2048, 16))
N_DIL = 3
KV_PER_BRANCH = 2
Q_PER_KV = 4
N_Q_HEADS = 24
N_KV_HEADS = 6
ATTN_OUT_DIM = 512
ROPE_DIM = 16
ROPE_THETA = 500000.0
PEER_HEADS = 8
PEER_N_KEYS = 128
PEER_TOPK = 16
PEER_HALF = 128
EPS = 1e-6

LANES = 128
SUBLANES = 8
VMEM_LIMIT_BYTES = 56 * 1024 * 1024

NEG_INF = float("-inf")


def _cparams(sem):
    return pltpu.CompilerParams(dimension_semantics=sem, vmem_limit_bytes=VMEM_LIMIT_BYTES)


def _dot_nt(a, b):
    return lax.dot_general(a, b, (((1,), (1,)), ((), ())), preferred_element_type=f32)


def _dot_tn(a, b):
    return lax.dot_general(a, b, (((0,), (0,)), ((), ())), preferred_element_type=f32)


def _sigmoid(x):
    return 1.0 / (1.0 + jnp.exp(-x))


def _silu(x):
    return x * _sigmoid(x)


def _softplus(x):
    return jnp.maximum(x, 0.0) + jnp.log1p(jnp.exp(-jnp.abs(x)))


def _gelu_tanh(x):
    c = math.sqrt(2.0 / math.pi)
    return 0.5 * x * (1.0 + jnp.tanh(c * (x + 0.044715 * (x * x * x))))


def _fmm_kernel(*refs, has_norm, mode):
    it = iter(refs)
    x_ref = next(it)
    g_ref = next(it) if has_norm else None
    w_ref = next(it)
    res_ref = next(it) if mode in ("add", "ple") else None
    if mode == "ple":
        p_ref, wp_ref = next(it), next(it)
    if mode == "headrope":
        hg_ref, ta_ref, tb_ref, tc_ref, bd_ref = next(it), next(it), next(it), next(it), next(it)
    o_ref = next(it)
    xn_ref = next(it)

    @pl.when(pl.program_id(1) == 0)
    def _():
        x = x_ref[...].astype(f32)
        if has_norm:
            ms = jnp.mean(x * x, axis=-1, keepdims=True)
            x = x * lax.rsqrt(ms + EPS) * g_ref[...]
        xn_ref[...] = x.astype(bf16)

    acc = jnp.dot(xn_ref[...], w_ref[...], preferred_element_type=f32)
    if mode == "plain":
        o_ref[...] = acc.astype(o_ref.dtype)
    elif mode == "add":
        o_ref[...] = res_ref[...] + acc
    elif mode == "ple":
        pp = jnp.dot(p_ref[...].astype(bf16), wp_ref[...], preferred_element_type=f32)
        o_ref[...] = res_ref[...] + pp * _sigmoid(acc)
    elif mode == "headrope":
        tn = acc.shape[1]
        ta, tb, tc = ta_ref[...], tb_ref[...], tc_ref[...]
        hg = hg_ref[...]
        bd = bd_ref[...]
        for nb in range(tn // LANES):
            blk = acc[:, nb * LANES:(nb + 1) * LANES]
            ms = jnp.dot(blk * blk, bd, precision=lax.Precision.HIGHEST, preferred_element_type=f32)
            y = blk * lax.rsqrt(ms + EPS) * hg
            y = y * ta + pltpu.roll(y, LANES - ROPE_DIM // 2, 1) * tb + pltpu.roll(y, ROPE_DIM // 2, 1) * tc
            o_ref[:, nb * LANES:(nb + 1) * LANES] = y


def fused_matmul(x, w, *, gain=None, mode="plain", res=None, ple=None, headrope=None, tm, tn, out_dtype=f32):
    m, k = x.shape
    n = w.shape[1]
    assert m % tm == 0 and n % tn == 0, (m, tm, n, tn)
    has_norm = gain is not None
    args = [x]
    in_specs = [pl.BlockSpec((tm, k), lambda i, j: (i, 0))]
    if has_norm:
        args.append(gain.reshape(1, k).astype(f32))
        in_specs.append(pl.BlockSpec((1, k), lambda i, j: (0, 0)))
    args.append(w)
    in_specs.append(pl.BlockSpec((k, tn), lambda i, j: (0, j)))
    if mode in ("add", "ple"):
        args.append(res)
        in_specs.append(pl.BlockSpec((tm, tn), lambda i, j: (i, j)))
    if mode == "ple":
        p, wp = ple
        kp = p.shape[1]
        args += [p, wp]
        in_specs += [pl.BlockSpec((tm, kp), lambda i, j: (i, 0)), pl.BlockSpec((kp, tn), lambda i, j: (0, j))]
    if mode == "headrope":
        hg, ta, tb, tc = headrope
        period = ta.shape[0] // tm
        args += [hg, ta, tb, tc, _head_mean_matrix()]
        in_specs.append(pl.BlockSpec((1, LANES), lambda i, j: (0, 0)))
        for _ in range(3):
            in_specs.append(pl.BlockSpec((tm, LANES), lambda i, j: (i % period, 0)))
        in_specs.append(pl.BlockSpec((LANES, LANES), lambda i, j: (0, 0)))
    return pl.pallas_call(
        functools.partial(_fmm_kernel, has_norm=has_norm, mode=mode),
        out_shape=jax.ShapeDtypeStruct((m, n), out_dtype),
        grid=(m // tm, n // tn),
        in_specs=in_specs,
        out_specs=pl.BlockSpec((tm, tn), lambda i, j: (i, j)),
        scratch_shapes=[pltpu.VMEM((tm, k), bf16)],
        compiler_params=_cparams(("parallel", "arbitrary")),
        name="fused_matmul_" + mode,
    )(*args)


def _head_mean_matrix():
    r = jnp.arange(LANES) // HEAD_DIM
    return (r[:, None] == r[None, :]).astype(f32) / HEAD_DIM


def _rope_tables(pos):
    half = ROPE_DIM // 2
    inv = ROPE_THETA ** (-jnp.arange(0, ROPE_DIM, 2, dtype=f32) / ROPE_DIM)
    ang = pos.astype(f32)[:, None] * inv[None, :]
    cos, sin = jnp.cos(ang), jnp.sin(ang)
    t = pos.shape[0]
    ones = jnp.ones((t, HEAD_DIM - ROPE_DIM), f32)
    zeros_h = jnp.zeros((t, half), f32)
    zeros_r = jnp.zeros((t, HEAD_DIM - ROPE_DIM), f32)
    ta = jnp.concatenate([cos, cos, ones], axis=1)
    tb = jnp.concatenate([-sin, zeros_h, zeros_r], axis=1)
    tc = jnp.concatenate([zeros_h, sin, zeros_r], axis=1)
    rep = LANES // HEAD_DIM
    return tuple(jnp.tile(a, (1, rep)) for a in (ta, tb, tc))


def _ssd_kernel(z_ref, xbc_ref, dt_ref, h0_ref, cb0_ref, cw_ref, cbias_ref, dtb_ref, alog_ref, dsk_ref, ng_ref,
                y_ref, hout_ref, h_scr, tail_scr, xc_scr, *, chunk, t_valid):
    c = pl.program_id(1)
    nc = pl.num_programs(1)
    L = chunk
    hp = SSM_HEADS_PER_GROUP * SSM_HEAD_DIM

    @pl.when(c == 0)
    def _():
        h_scr[...] = h0_ref[...]
        tail_scr[...] = cb0_ref[...]

    x = xbc_ref[...]
    tail = tail_scr[...]
    w = cw_ref[...]
    acc = x * w[CONV_W - 1:CONV_W, :] + cbias_ref[...]
    row8 = lax.broadcasted_iota(jnp.int32, (SUBLANES, CONV_DIM), 0)
    for s in range(1, CONV_W):
        xr = pltpu.roll(x, s, 0)
        tr = pltpu.roll(tail, s, 0)
        head = jnp.where(row8 < s, tr, xr[0:SUBLANES])
        shifted = jnp.concatenate([head, xr[SUBLANES:]], axis=0)
        acc = acc + shifted * w[CONV_W - 1 - s:CONV_W - s, :]
    tail_scr[...] = x[L - SUBLANES:L]
    xc_scr[...] = _silu(acc)

    dt = _softplus(dt_ref[...] + dtb_ref[...])
    if t_valid is not None:
        rowid = c * L + lax.broadcasted_iota(jnp.int32, (L, LANES), 0)
        dt = jnp.where(rowid < t_valid, dt, 0.0)
    a = -jnp.exp(alog_ref[...])
    da = dt * a
    r_i = lax.broadcasted_iota(jnp.int32, (L, L), 0)
    c_i = lax.broadcasted_iota(jnp.int32, (L, L), 1)
    causal = r_i >= c_i
    cum = jnp.dot(causal.astype(f32), da, precision=lax.Precision.HIGHEST, preferred_element_type=f32)
    cum_t = cum.T
    cum_last = cum[L - 1:L, :]
    tailw = jnp.exp(cum_last - cum) * dt
    ecum = jnp.exp(cum)
    dsk = dsk_ref[...]

    for g in range(SSM_GROUPS):
        b_g = xc_scr[:, D_INNER + g * D_STATE:D_INNER + (g + 1) * D_STATE].astype(bf16)
        c_g = xc_scr[:, D_INNER + SSM_GROUPS * D_STATE + g * D_STATE:
                     D_INNER + SSM_GROUPS * D_STATE + (g + 1) * D_STATE].astype(bf16)
        cb = _dot_nt(c_g, b_g)
        h_g = h_scr[g * hp:(g + 1) * hp, :]
        y_in = _dot_nt(c_g, h_g.astype(bf16))
        ys, ws, scales = [], [], []
        for hh in range(SSM_HEADS_PER_GROUP):
            h = g * SSM_HEADS_PER_GROUP + hh
            xh = xc_scr[:, h * SSM_HEAD_DIM:(h + 1) * SSM_HEAD_DIM]
            seg = cum[:, h:h + 1] - cum_t[h:h + 1, :]
            dec = jnp.exp(jnp.where(causal, seg, NEG_INF))
            m = (cb * dec).astype(bf16)
            xdt = (xh * dt[:, h:h + 1]).astype(bf16)
            yh = jnp.dot(m, xdt, preferred_element_type=f32)
            yh = yh + y_in[:, hh * SSM_HEAD_DIM:(hh + 1) * SSM_HEAD_DIM] * ecum[:, h:h + 1]
            yh = yh + xh * dsk[:, h:h + 1]
            ys.append(yh)
            ws.append(xh * tailw[:, h:h + 1])
            scales.append(jnp.broadcast_to(jnp.exp(cum_t[h:h + 1, L - 1:L]), (SSM_HEAD_DIM, D_STATE)))
        wcat = jnp.concatenate(ws, axis=1).astype(bf16)
        upd = _dot_tn(wcat, b_g)
        h_scr[g * hp:(g + 1) * hp, :] = h_g * jnp.concatenate(scales, axis=0) + upd
        yg = jnp.concatenate(ys, axis=1)
        yg = yg * _silu(z_ref[:, g * hp:(g + 1) * hp])
        yg = yg * lax.rsqrt(jnp.mean(yg * yg, axis=-1, keepdims=True) + EPS)
        y_ref[:, g * hp:(g + 1) * hp] = (yg * ng_ref[:, g * hp:(g + 1) * hp]).astype(y_ref.dtype)

    @pl.when(c == nc - 1)
    def _():
        hout_ref[...] = h_scr[...]


def ssd_mixer(z, xbc, dtr, h0, cb0, conv_w, conv_b, dt_bias, a_log, d_skip, norm_g, *, t_valid):
    b, t, _ = z.shape
    L = SSD_CHUNK
    assert t % L == 0
    nc = t // L
    hp_all = SSM_HEADS * SSM_HEAD_DIM
    pad_h = LANES - SSM_HEADS

    def lane_pad(v):
        return jnp.pad(v.astype(f32), (0, pad_h)).reshape(1, LANES)

    cw = jnp.pad(conv_w.astype(f32), ((0, SUBLANES - CONV_W), (0, 0)))
    const = lambda shape: pl.BlockSpec(shape, lambda i, j: (0,) * len(shape))
    return pl.pallas_call(
        functools.partial(_ssd_kernel, chunk=L, t_valid=t_valid),
        out_shape=(jax.ShapeDtypeStruct((b, t, D_INNER), bf16),
                   jax.ShapeDtypeStruct((b, hp_all, D_STATE), f32)),
        grid=(b, nc),
        in_specs=[
            pl.BlockSpec((None, L, D_INNER), lambda i, j: (i, j, 0)),
            pl.BlockSpec((None, L, CONV_DIM), lambda i, j: (i, j, 0)),
            pl.BlockSpec((None, L, LANES), lambda i, j: (i, j, 0)),
            pl.BlockSpec((None, hp_all, D_STATE), lambda i, j: (i, 0, 0)),
            pl.BlockSpec((None, SUBLANES, CONV_DIM), lambda i, j: (i, 0, 0)),
            const((SUBLANES, CONV_DIM)),
            const((1, CONV_DIM)),
            const((1, LANES)), const((1, LANES)), const((1, LANES)),
            const((1, D_INNER)),
        ],
        out_specs=(pl.BlockSpec((None, L, D_INNER), lambda i, j: (i, j, 0)),
                   pl.BlockSpec((None, hp_all, D_STATE), lambda i, j: (i, 0, 0))),
        scratch_shapes=[pltpu.VMEM((hp_all, D_STATE), f32),
                        pltpu.VMEM((SUBLANES, CONV_DIM), f32),
                        pltpu.VMEM((L, CONV_DIM), f32)],
        compiler_params=_cparams(("parallel", "arbitrary")),
        name="ssd_mixer",
    )(z, xbc, dtr, h0, cb0, cw, conv_b.reshape(1, CONV_DIM).astype(f32),
      lane_pad(dt_bias), lane_pad(a_log), lane_pad(d_skip), norm_g.reshape(1, D_INNER).astype(f32))


def _extract_top(s, n, want_rank):
    vals = []
    rank = jnp.full(s.shape, float(n), f32) if want_rank else None
    for r in range(n):
        m = jnp.max(s, axis=0, keepdims=True)
        vals.append(m)
        hit = s == m
        if want_rank:
            rank = jnp.where(hit, float(r), rank)
        s = jnp.where(hit, NEG_INF, s)
    return vals, rank


def _peer_scores_kernel(h_ref, g_ref, wq_ref, sk_ref, xn_ref, rank_ref, n1_ref, c1_ref, e2_ref, q_scr):
    x = h_ref[...]
    ms = jnp.mean(x * x, axis=-1, keepdims=True)
    xn = (x * lax.rsqrt(ms + EPS) * g_ref[...]).astype(bf16)
    xn_ref[...] = xn
    q = jnp.dot(xn, wq_ref[...], preferred_element_type=f32).astype(bf16)
    for i in range(2 * PEER_HEADS):
        q_scr[i] = q[:, i * PEER_HALF:(i + 1) * PEER_HALF]
    kk = PEER_TOPK
    sk0 = sk_ref[0]
    sk1 = sk_ref[1]

    def head(h, carry):
        s1 = _dot_nt(sk0, q_scr[2 * h])
        s2 = _dot_nt(sk1, q_scr[2 * h + 1])
        av, _ = _extract_top(s1, kk, False)
        bv, rank2 = _extract_top(s2, kk, True)
        cands = []
        for r in range(kk):
            for cc in range(kk // (r + 1)):
                cands.append(av[r] + bv[cc])
        pad = (-len(cands)) % SUBLANES
        cands += [jnp.full_like(cands[0], NEG_INF)] * pad
        best, _ = _extract_top(jnp.concatenate(cands, axis=0), kk, False)
        tau = best[kk - 1]
        top = best[0]
        zsum = jnp.zeros_like(top)
        for r in range(kk):
            zsum = zsum + jnp.exp(best[r] - top)
        n1 = jnp.zeros(s1.shape, f32)
        for cc in range(kk):
            n1 = n1 + jnp.where(s1 + bv[cc] >= tau, 1.0, 0.0)
        rank_ref[h] = rank2.astype(rank_ref.dtype)
        n1_ref[h] = n1.astype(n1_ref.dtype)
        c1_ref[h] = (jnp.exp(s1 - av[0]) / zsum).astype(c1_ref.dtype)
        e2_ref[h] = jnp.exp(s2 - bv[0]).astype(e2_ref.dtype)
        return carry

    lax.fori_loop(0, PEER_HEADS, head, 0)


def peer_scores(h, gain, wq, sk, *, tm, sel_dtype):
    n, d = h.shape
    nq = wq.shape[1]
    sel = jax.ShapeDtypeStruct((PEER_HEADS, PEER_N_KEYS, n), sel_dtype)
    sel_spec = pl.BlockSpec((PEER_HEADS, PEER_N_KEYS, tm), lambda i: (0, 0, i))
    return pl.pallas_call(
        _peer_scores_kernel,
        out_shape=(jax.ShapeDtypeStruct((n, d), bf16), sel, sel, sel, sel),
        grid=(n // tm,),
        in_specs=[pl.BlockSpec((tm, d), lambda i: (i, 0)),
                  pl.BlockSpec((1, d), lambda i: (0, 0)),
                  pl.BlockSpec((d, nq), lambda i: (0, 0)),
                  pl.BlockSpec((2, PEER_N_KEYS, PEER_HALF), lambda i: (0, 0, 0))],
        out_specs=(pl.BlockSpec((tm, d), lambda i: (i, 0)), sel_spec, sel_spec, sel_spec, sel_spec),
        scratch_shapes=[pltpu.VMEM((2 * PEER_HEADS, tm, PEER_HALF), bf16)],
        compiler_params=_cparams(("parallel",)),
        name="peer_scores",
    )(h, gain.reshape(1, d).astype(f32), wq, sk)


def _peer_dense_kernel(xn_ref, u_ref, vt_ref, rank_ref, n1_ref, c1_ref, e2_ref, h_ref, o_ref,
                       acc_scr, act_scr, coef_scr, *, te):
    e = pl.program_id(1)
    ne = pl.num_programs(1)
    nk = PEER_N_KEYS
    a_per_step = te // nk

    @pl.when(e == 0)
    def _():
        acc_scr[...] = jnp.zeros_like(acc_scr)

    act_scr[...] = _dot_nt(u_ref[...], xn_ref[...])

    def a_block(ai, carry):
        a = e * a_per_step + ai
        off = pl.multiple_of(ai * nk, nk)
        g = _gelu_tanh(act_scr[pl.ds(off, nk), :])
        wsum = None
        for h in range(PEER_HEADS):
            n1a = n1_ref[h, pl.ds(a, 1), :]
            c1a = c1_ref[h, pl.ds(a, 1), :]
            term = jnp.where(rank_ref[h] < n1a, e2_ref[h], jnp.zeros((), e2_ref.dtype)) * c1a
            wsum = term if wsum is None else wsum + term
        coef_scr[pl.ds(off, nk), :] = (wsum.astype(f32) * g).astype(bf16)
        return carry

    lax.fori_loop(0, a_per_step, a_block, 0)
    acc_scr[...] += jnp.dot(vt_ref[...], coef_scr[...], preferred_element_type=f32)

    @pl.when(e == ne - 1)
    def _():
        o_ref[...] = h_ref[...] + acc_scr[...].T


def peer_dense(xn, u, vt, rank2, n1, c1, e2, h, *, tm, te):
    n, d = h.shape
    n_exp = u.shape[0]
    sel_spec = pl.BlockSpec((PEER_HEADS, PEER_N_KEYS, tm), lambda i, j: (0, 0, i))
    return pl.pallas_call(
        functools.partial(_peer_dense_kernel, te=te),
        out_shape=jax.ShapeDtypeStruct((n, d), f32),
        grid=(n // tm, n_exp // te),
        in_specs=[pl.BlockSpec((tm, d), lambda i, j: (i, 0)),
                  pl.BlockSpec((te, d), lambda i, j: (j, 0)),
                  pl.BlockSpec((d, te), lambda i, j: (0, j)),
                  sel_spec, sel_spec, sel_spec, sel_spec,
                  pl.BlockSpec((tm, d), lambda i, j: (i, 0))],
        out_specs=pl.BlockSpec((tm, d), lambda i, j: (i, 0)),
        scratch_shapes=[pltpu.VMEM((d, tm), f32), pltpu.VMEM((te, tm), f32), pltpu.VMEM((te, tm), bf16)],
        compiler_params=_cparams(("parallel", "arbitrary")),
        name="peer_dense",
    )(xn, u, vt, rank2, n1, c1, e2, h)


def _attn_band_kernel(q_ref, kc_ref, kp_ref, vc_ref, vp_ref, o_ref, l_ref):
    i = pl.program_id(2)
    blk = q_ref.shape[0]
    scale = HEAD_DIM ** -0.5
    q = q_ref[...] * scale
    kc, kp, vc, vp = kc_ref[...], kp_ref[...], vc_ref[...], vp_ref[...]
    rows = Q_PER_KV * blk
    a_i = lax.broadcasted_iota(jnp.int32, (rows, blk), 0) % blk
    c_i = lax.broadcasted_iota(jnp.int32, (rows, blk), 1)
    mask_c = c_i <= a_i
    mask_p = jnp.logical_and(c_i >= a_i, i > 0)
    outs, lses = [], []
    for j in range(KV_PER_BRANCH):
        hs = slice(j * HEAD_DIM, (j + 1) * HEAD_DIM)
        qs = jnp.concatenate([q[:, (j * Q_PER_KV + r) * HEAD_DIM:(j * Q_PER_KV + r + 1) * HEAD_DIM]
                              for r in range(Q_PER_KV)], axis=0).astype(bf16)
        s_c = jnp.where(mask_c, _dot_nt(qs, kc[:, hs].astype(bf16)), NEG_INF)
        s_p = jnp.where(mask_p, _dot_nt(qs, kp[:, hs].astype(bf16)), NEG_INF)
        m = jnp.maximum(jnp.max(s_c, axis=-1, keepdims=True), jnp.max(s_p, axis=-1, keepdims=True))
        p_c = jnp.exp(s_c - m)
        p_p = jnp.exp(s_p - m)
        l = jnp.sum(p_c, axis=-1, keepdims=True) + jnp.sum(p_p, axis=-1, keepdims=True)
        o = jnp.dot(p_c.astype(bf16), vc[:, hs].astype(bf16), preferred_element_type=f32)
        o = o + jnp.dot(p_p.astype(bf16), vp[:, hs].astype(bf16), preferred_element_type=f32)
        o = o / l
        lse = jnp.broadcast_to(m + jnp.log(l), (rows, HEAD_DIM))
        for r in range(Q_PER_KV):
            outs.append(o[r * blk:(r + 1) * blk])
            lses.append(lse[r * blk:(r + 1) * blk])
    o_ref[...] = jnp.concatenate(outs, axis=1)
    l_ref[...] = jnp.concatenate(lses, axis=1)


def attn_band(q, k, v, g, dil):
    b, t, _ = q.shape
    s = t // dil
    blk = 128
    assert s % blk == 0
    qc = N_Q_HEADS * HEAD_DIM
    kc = N_KV_HEADS * HEAD_DIM
    hq = Q_PER_KV * KV_PER_BRANCH * HEAD_DIM
    hk = KV_PER_BRANCH * HEAD_DIM
    qv = q.reshape(b, s, dil * qc)
    kv_ = k.reshape(b, s, dil * kc)
    vv = v.reshape(b, s, dil * kc)
    nqb, nkb = qc // hq, kc // hk
    cur = lambda bi, r, i: (bi, i, r * nkb + g)
    prev = lambda bi, r, i: (bi, jnp.maximum(i - 1, 0), r * nkb + g)
    osd = jax.ShapeDtypeStruct((b, s, dil * hq), f32)
    o, l = pl.pallas_call(
        _attn_band_kernel,
        out_shape=(osd, osd),
        grid=(b, dil, s // blk),
        in_specs=[pl.BlockSpec((None, blk, hq), lambda bi, r, i: (bi, i, r * nqb + g)),
                  pl.BlockSpec((None, blk, hk), cur), pl.BlockSpec((None, blk, hk), prev),
                  pl.BlockSpec((None, blk, hk), cur), pl.BlockSpec((None, blk, hk), prev)],
        out_specs=(pl.BlockSpec((None, blk, hq), lambda bi, r, i: (bi, i, r)),
                   pl.BlockSpec((None, blk, hq), lambda bi, r, i: (bi, i, r))),
        compiler_params=_cparams(("parallel", "parallel", "arbitrary")),
        name=f"attn_band_d{dil}",
    )(qv, kv_, kv_, vv, vv)
    return o.reshape(b, t, hq), l.reshape(b, t, hq)


def _merge3_kernel(o0, o1, o2, l0, l1, l2, out_ref):
    a0, a1, a2 = l0[...], l1[...], l2[...]
    m = jnp.maximum(jnp.maximum(a0, a1), a2)
    w0, w1, w2 = jnp.exp(a0 - m), jnp.exp(a1 - m), jnp.exp(a2 - m)
    out_ref[...] = ((w0 * o0[...] + w1 * o1[...] + w2 * o2[...]) / (w0 + w1 + w2)).astype(out_ref.dtype)


def merge3(os_, ls_, *, tm):
    n, c = os_[0].shape
    spec = pl.BlockSpec((tm, c), lambda i: (i, 0))
    return pl.pallas_call(
        _merge3_kernel,
        out_shape=jax.ShapeDtypeStruct((n, c), bf16),
        grid=(n // tm,),
        in_specs=[spec] * 6,
        out_specs=spec,
        compiler_params=_cparams(("parallel",)),
        name="attn_merge",
    )(*os_, *ls_)


def _attn_sample_kernel(q_ref, ck_ref, cv_ref, nk_ref, nv_ref, o_ref, *, t_new, win_buf):
    scale = HEAD_DIM ** -0.5
    q = q_ref[...] * scale
    n_cache = ck_ref.shape[0]
    n_new = nk_ref.shape[0]
    rows = Q_PER_KV * t_new
    t_c = lax.broadcasted_iota(jnp.int32, (rows, n_cache), 0) % t_new
    d_c = win_buf + t_c - lax.broadcasted_iota(jnp.int32, (rows, n_cache), 1)
    t_n = lax.broadcasted_iota(jnp.int32, (rows, n_new), 0) % t_new
    d_n = t_n - lax.broadcasted_iota(jnp.int32, (rows, n_new), 1)
    outs = [None] * (KV_PER_BRANCH * Q_PER_KV)
    for j in range(KV_PER_BRANCH):
        parts = []
        for g, (win, dil) in enumerate(DIL_PATTERNS):
            hq = (g * KV_PER_BRANCH + j) * Q_PER_KV
            hk = slice((g * KV_PER_BRANCH + j) * HEAD_DIM, (g * KV_PER_BRANCH + j + 1) * HEAD_DIM)
            qs = jnp.concatenate([q[:, (hq + r) * HEAD_DIM:(hq + r + 1) * HEAD_DIM] for r in range(Q_PER_KV)],
                                 axis=0).astype(bf16)
            ok_c = (d_c >= 0) & (d_c <= win) & ((d_c & (dil - 1)) == 0)
            ok_n = (d_n >= 0) & (d_n <= win) & ((d_n & (dil - 1)) == 0)
            s_c = jnp.where(ok_c, _dot_nt(qs, ck_ref[:, hk].astype(bf16)), NEG_INF)
            s_n = jnp.where(ok_n, _dot_nt(qs, nk_ref[:, hk].astype(bf16)), NEG_INF)
            parts.append((s_c, s_n, hk))
        m = None
        for s_c, s_n, _ in parts:
            mm = jnp.maximum(jnp.max(s_c, axis=-1, keepdims=True), jnp.max(s_n, axis=-1, keepdims=True))
            m = mm if m is None else jnp.maximum(m, mm)
        l = jnp.zeros_like(m)
        o = jnp.zeros((rows, HEAD_DIM), f32)
        for s_c, s_n, hk in parts:
            p_c = jnp.exp(s_c - m)
            p_n = jnp.exp(s_n - m)
            l = l + jnp.sum(p_c, axis=-1, keepdims=True) + jnp.sum(p_n, axis=-1, keepdims=True)
            o = o + jnp.dot(p_c.astype(bf16), cv_ref[:, hk].astype(bf16), preferred_element_type=f32)
            o = o + jnp.dot(p_n.astype(bf16), nv_ref[:, hk].astype(bf16), preferred_element_type=f32)
        o = o / l
        for r in range(Q_PER_KV):
            outs[j * Q_PER_KV + r] = o[r * t_new:(r + 1) * t_new]
    o_ref[...] = jnp.concatenate(outs, axis=1).astype(o_ref.dtype)


def attn_sample(q, ck, cv, nk, nv):
    b, t_new, qc = q.shape
    w = ck.shape[1]
    kc = ck.shape[2]
    n_new = nk.shape[1]
    return pl.pallas_call(
        functools.partial(_attn_sample_kernel, t_new=t_new, win_buf=w),
        out_shape=jax.ShapeDtypeStruct((b, t_new, ATTN_OUT_DIM), bf16),
        grid=(b,),
        in_specs=[pl.BlockSpec((None, t_new, qc), lambda i: (i, 0, 0)),
                  pl.BlockSpec((None, w, kc), lambda i: (i, 0, 0)),
                  pl.BlockSpec((None, w, kc), lambda i: (i, 0, 0)),
                  pl.BlockSpec((None, n_new, kc), lambda i: (i, 0, 0)),
                  pl.BlockSpec((None, n_new, kc), lambda i: (i, 0, 0))],
        out_specs=pl.BlockSpec((None, t_new, ATTN_OUT_DIM), lambda i: (i, 0, 0)),
        compiler_params=_cparams(("parallel",)),
        name="attn_sample",
    )(q, ck, cv, nk, nv)


def _prep_weights(norm_mix, norm_ffn, norm_ple, ssm_w_in, ssm_conv_w, ssm_conv_b, ssm_dt_bias, ssm_a_log, ssm_d,
                  ssm_norm, ssm_w_out, kv_norm, w_kv, k_norm, w_q, q_norm, w_o,
                  peer_w_query, peer_sub_keys, peer_u, peer_v, ple_w_proj, ple_w_gate):
    zx = D_INNER + CONV_DIM
    w_in = ssm_w_in[0]
    kdim = N_KV_HEADS * HEAD_DIM
    rep = LANES // HEAD_DIM
    return dict(
        norm_mix=norm_mix, norm_ffn=norm_ffn, norm_ple=norm_ple,
        w_z=w_in[:, :D_INNER].astype(bf16), w_xbc=w_in[:, D_INNER:zx].astype(bf16),
        w_dt=jnp.pad(w_in[:, zx:], ((0, 0), (0, LANES - SSM_HEADS))).astype(bf16),
        conv_w=ssm_conv_w[0], conv_b=ssm_conv_b[0], dt_bias=ssm_dt_bias[0], a_log=ssm_a_log[0],
        d_skip=ssm_d[0], ssm_norm=ssm_norm[0], w_out=ssm_w_out[0].astype(bf16),
        kv_norm=kv_norm, w_k=w_kv[:, :kdim].astype(bf16), w_v=w_kv[:, kdim:].astype(bf16),
        k_gain=jnp.tile(k_norm.astype(f32), rep).reshape(1, LANES),
        w_q=w_q[0].astype(bf16), q_gain=jnp.tile(q_norm[0].astype(f32), rep).reshape(1, LANES),
        w_o=w_o[0].astype(bf16),
        peer_wq=[peer_w_query[i].astype(bf16) for i in range(DEPTH)],
        peer_sk=[peer_sub_keys[i].astype(bf16) for i in range(DEPTH)],
        peer_u=[peer_u[i].astype(bf16) for i in range(DEPTH)],
        peer_vt=[peer_v[i].astype(bf16).T for i in range(DEPTH)],
        ple_wp=[ple_w_proj[i].astype(bf16) for i in range(DEPTH)],
        ple_wg=[ple_w_gate[i].astype(bf16) for i in range(DEPTH)],
    )


def _peer_ple(h, p, i, wts, tm, tm_s, te):
    xn, rank2, n1, c1, e2 = peer_scores(h, wts["norm_ffn"][i], wts["peer_wq"][i], wts["peer_sk"][i],
                                        tm=tm_s, sel_dtype=f32)
    h = peer_dense(xn, wts["peer_u"][i], wts["peer_vt"][i], rank2, n1, c1, e2, h, tm=tm, te=te)
    return fused_matmul(h, wts["ple_wg"][i], gain=wts["norm_ple"][i], mode="ple", res=h,
                        ple=(p, wts["ple_wp"][i]), tm=tm, tn=512)


def _trunk(x, p, conv_state, ssm_state, past_k, past_v, pos0, wts):
    b, t, d = x.shape
    n = b * t
    prompt = past_k is None
    tm = 512 if n % 512 == 0 else n
    tm_s = 256
    te = 1024
    h = x.reshape(n, d)

    z = fused_matmul(h, wts["w_z"], gain=wts["norm_mix"][0], tm=tm, tn=512).reshape(b, t, D_INNER)
    xbc = fused_matmul(h, wts["w_xbc"], gain=wts["norm_mix"][0], tm=tm, tn=512).reshape(b, t, CONV_DIM)
    dtr = fused_matmul(h, wts["w_dt"], gain=wts["norm_mix"][0], tm=tm, tn=LANES)
    new_conv = xbc[:, t - (CONV_W - 1):, :] if t >= CONV_W - 1 else None
    dtr = dtr.reshape(b, t, LANES)
    tp = -(-t // SSD_CHUNK) * SSD_CHUNK
    if tp != t:
        padt = ((0, 0), (0, tp - t), (0, 0))
        z_p, xbc_p, dtr_p = jnp.pad(z, padt), jnp.pad(xbc, padt), jnp.pad(dtr, padt)
    else:
        z_p, xbc_p, dtr_p = z, xbc, dtr
    cb0 = jnp.pad(conv_state.astype(f32), ((0, 0), (SUBLANES - (CONV_W - 1), 0), (0, 0)))
    h0 = ssm_state.astype(f32).reshape(b, SSM_HEADS * SSM_HEAD_DIM, D_STATE)
    y, h_fin = ssd_mixer(z_p, xbc_p, dtr_p, h0, cb0, wts["conv_w"], wts["conv_b"], wts["dt_bias"], wts["a_log"],
                         wts["d_skip"], wts["ssm_norm"], t_valid=(None if tp == t else t))
    y = y[:, :t].reshape(n, D_INNER)
    h = fused_matmul(y, wts["w_out"], mode="add", res=h, tm=tm, tn=512)
    new_ssm = h_fin.reshape(b, SSM_HEADS, SSM_HEAD_DIM, D_STATE)
    h = _peer_ple(h, p[0].reshape(n, -1), 0, wts, tm, tm_s, te)

    pos = pos0 + jnp.arange(t, dtype=jnp.int32)
    tabs = _rope_tables(pos)
    if t < tm:
        tabs = tuple(jnp.tile(a, (tm // t, 1)) for a in tabs)
    k_new = fused_matmul(h, wts["w_k"], gain=wts["kv_norm"], mode="headrope",
                         headrope=(wts["k_gain"],) + tabs, tm=tm, tn=N_KV_HEADS * HEAD_DIM)
    v_new = fused_matmul(h, wts["w_v"], gain=wts["kv_norm"], tm=tm, tn=N_KV_HEADS * HEAD_DIM)

    q = fused_matmul(h, wts["w_q"], gain=wts["norm_mix"][1], mode="headrope",
                     headrope=(wts["q_gain"],) + tabs, tm=tm, tn=512)
    kdim = N_KV_HEADS * HEAD_DIM
    if prompt:
        q3, k3, v3 = q.reshape(b, t, -1), k_new.reshape(b, t, kdim), v_new.reshape(b, t, kdim)
        os_, ls_ = [], []
        for g, (_, dil) in enumerate(DIL_PATTERNS):
            o_g, l_g = attn_band(q3, k3, v3, g, dil)
            os_.append(o_g.reshape(n, ATTN_OUT_DIM))
            ls_.append(l_g.reshape(n, ATTN_OUT_DIM))
        att = merge3(os_, ls_, tm=tm)
    else:
        padn = ((0, 0), (0, SSD_CHUNK - t), (0, 0))
        nk = jnp.pad(k_new.reshape(b, t, kdim), padn)
        nv = jnp.pad(v_new.reshape(b, t, kdim), padn)
        w = past_k.shape[1]
        att = attn_sample(q.reshape(b, t, -1), past_k.reshape(b, w, kdim).astype(f32),
                          past_v.reshape(b, w, kdim).astype(f32), nk, nv).reshape(n, ATTN_OUT_DIM)
    h = fused_matmul(att, wts["w_o"], mode="add", res=h, tm=tm, tn=512)
    h = _peer_ple(h, p[1].reshape(n, -1), 1, wts, tm, tm_s, te)

    return (h.reshape(b, t, d), new_conv[None], new_ssm[None],
            k_new.reshape(b, t, N_KV_HEADS, HEAD_DIM), v_new.reshape(b, t, N_KV_HEADS, HEAD_DIM))


def kernel(x_prompt, x_sample, state_conv, state_ssm, cache_k, cache_v, p_prompt, p_sample, norm_mix, norm_ffn, norm_ple, ssm_w_in, ssm_conv_w, ssm_conv_b, ssm_dt_bias, ssm_a_log, ssm_d, ssm_norm, ssm_w_out, kv_norm, w_kv, k_norm, w_q, q_norm, w_o, peer_w_query, peer_sub_keys, peer_u, peer_v, ple_w_proj, ple_w_gate):
    wts = _prep_weights(norm_mix, norm_ffn, norm_ple, ssm_w_in, ssm_conv_w, ssm_conv_b, ssm_dt_bias, ssm_a_log,
                        ssm_d, ssm_norm, ssm_w_out, kv_norm, w_kv, k_norm, w_q, q_norm, w_o,
                        peer_w_query, peer_sub_keys, peer_u, peer_v, ple_w_proj, ple_w_gate)
    b_p, t_p, _ = x_prompt.shape
    zero_conv = jnp.zeros((b_p, CONV_W - 1, CONV_DIM), x_prompt.dtype)
    zero_ssm = jnp.zeros((b_p, SSM_HEADS, SSM_HEAD_DIM, D_STATE), x_prompt.dtype)
    y_p, conv_p, ssm_p, k_p, v_p = _trunk(x_prompt, p_prompt, zero_conv, zero_ssm, None, None, 0, wts)
    keep = min(max(w for w, _ in DIL_PATTERNS), t_p)
    past_len = 16384
    y_s, conv_s, ssm_s, k_s, v_s = _trunk(x_sample, p_sample, state_conv[0], state_ssm[0], cache_k, cache_v,
                                          past_len, wts)
    return (y_p, y_s, conv_p, ssm_p, k_p[:, -keep:], v_p[:, -keep:], conv_s, ssm_s, k_s, v_s)
```

```python
import functools
import math

import jax
import jax.numpy as jnp
from jax import lax
from jax.experimental import pallas as pl
from jax.experimental.pallas import tpu as pltpu

f32 = jnp.float32
bf16 = jnp.bfloat16

D_MODEL = 1024
DEPTH = 2
N_A_LAYERS = 1
D_INNER = 2048
SSM_HEAD_DIM = 64
SSM_HEADS = 32
SSM_GROUPS = 8
SSM_HEADS_PER_GROUP = 4
D_STATE = 128
CONV_W = 4
CONV_DIM = 4096
SSD_CHUNK = 128
HEAD_DIM = 64
DIL_PATTERNS = ((128, 1), (512, 4), (2048, 16))
N_DIL = 3
KV_PER_BRANCH = 2
Q_PER_KV = 4
N_Q_HEADS = 24
N_KV_HEADS = 6
ATTN_OUT_DIM = 512
ROPE_DIM = 16
ROPE_THETA = 500000.0
PEER_HEADS = 8
PEER_N_KEYS = 128
PEER_TOPK = 16
PEER_HALF = 128
EPS = 1e-6

LANES = 128
SUBLANES = 8
VMEM_LIMIT_BYTES = 56 * 1024 * 1024

NEG_INF = float("-inf")


def _cparams(sem):
    return pltpu.CompilerParams(dimension_semantics=sem, vmem_limit_bytes=VMEM_LIMIT_BYTES)


def _dot_nt(a, b):
    return lax.dot_general(a, b, (((1,), (1,)), ((), ())), preferred_element_type=f32)


def _dot_tn(a, b):
    return lax.dot_general(a, b, (((0,), (0,)), ((), ())), preferred_element_type=f32)


def _sigmoid(x):
    return 1.0 / (1.0 + jnp.exp(-x))


def _silu(x):
    return x * _sigmoid(x)


def _softplus(x):
    return jnp.maximum(x, 0.0) + jnp.log1p(jnp.exp(-jnp.abs(x)))


def _gelu_tanh(x):
    c = math.sqrt(2.0 / math.pi)
    return x * (0.5 + 0.5 * jnp.tanh(x * (c + (c * 0.044715) * (x * x))))


def _fmm_kernel(*refs, has_norm, mode):
    it = iter(refs)
    x_ref = next(it)
    g_ref = next(it) if has_norm else None
    w_ref = next(it)
    res_ref = next(it) if mode in ("add", "ple") else None
    if mode == "ple":
        p_ref, wp_ref = next(it), next(it)
    if mode == "headrope":
        hg_ref, ta_ref, tb_ref, tc_ref, bd_ref = next(it), next(it), next(it), next(it), next(it)
    o_ref = next(it)
    xn_ref = next(it)

    @pl.when(pl.program_id(1) == 0)
    def _():
        x = x_ref[...].astype(f32)
        if has_norm:
            ms = jnp.mean(x * x, axis=-1, keepdims=True)
            x = x * lax.rsqrt(ms + EPS) * g_ref[...]
        xn_ref[...] = x.astype(bf16)

    acc = jnp.dot(xn_ref[...], w_ref[...], preferred_element_type=f32)
    if mode == "plain":
        o_ref[...] = acc.astype(o_ref.dtype)
    elif mode == "add":
        o_ref[...] = res_ref[...] + acc
    elif mode == "ple":
        pp = jnp.dot(p_ref[...].astype(bf16), wp_ref[...], preferred_element_type=f32)
        o_ref[...] = res_ref[...] + pp * _sigmoid(acc)
    elif mode == "headrope":
        tn = acc.shape[1]
        ta, tb, tc = ta_ref[...], tb_ref[...], tc_ref[...]
        hg = hg_ref[...]
        bd = bd_ref[...]
        for nb in range(tn // LANES):
            blk = acc[:, nb * LANES:(nb + 1) * LANES]
            ms = jnp.dot(blk * blk, bd, precision=lax.Precision.HIGHEST, preferred_element_type=f32)
            y = blk * lax.rsqrt(ms + EPS) * hg
            y = y * ta + pltpu.roll(y, LANES - ROPE_DIM // 2, 1) * tb + pltpu.roll(y, ROPE_DIM // 2, 1) * tc
            o_ref[:, nb * LANES:(nb + 1) * LANES] = y


MATMUL_VMEM_BUDGET = 40 * 1024 * 1024


def _token_tile(m):
    return next((c for c in (1024, 512, 256, 128) if m % c == 0), m)


def _col_tile(tm, k, n, x_bytes, mode, kp):
    out_blocks = 2 if mode in ("add", "ple") else 1
    for tn in sorted({n, 2048, 1536, 1024, 768, 512, 384, 256, 128}, reverse=True):
        if tn > n or n % tn:
            continue
        use = 2 * tm * k * x_bytes + tm * k * 2 + 2 * k * tn * 2 + 2 * out_blocks * tm * tn * 4
        if mode == "ple":
            use += 2 * tm * kp * 4 + 2 * kp * tn * 2
        if use <= MATMUL_VMEM_BUDGET:
            return tn
    raise ValueError("no column tile fits")


def fused_matmul(x, w, *, gain=None, mode="plain", res=None, ple=None, headrope=None, tm, tn=None, out_dtype=f32):
    m, k = x.shape
    n = w.shape[1]
    if tn is None:
        tn = _col_tile(tm, k, n, x.dtype.itemsize, mode, ple[0].shape[1] if ple else 0)
    assert m % tm == 0 and n % tn == 0, (m, tm, n, tn)
    has_norm = gain is not None
    args = [x]
    in_specs = [pl.BlockSpec((tm, k), lambda i, j: (i, 0))]
    if has_norm:
        args.append(gain.reshape(1, k).astype(f32))
        in_specs.append(pl.BlockSpec((1, k), lambda i, j: (0, 0)))
    args.append(w)
    in_specs.append(pl.BlockSpec((k, tn), lambda i, j: (0, j)))
    if mode in ("add", "ple"):
        args.append(res)
        in_specs.append(pl.BlockSpec((tm, tn), lambda i, j: (i, j)))
    if mode == "ple":
        p, wp = ple
        kp = p.shape[1]
        args += [p, wp]
        in_specs += [pl.BlockSpec((tm, kp), lambda i, j: (i, 0)), pl.BlockSpec((kp, tn), lambda i, j: (0, j))]
    if mode == "headrope":
        hg, ta, tb, tc = headrope
        period = ta.shape[0] // tm
        args += [hg, ta, tb, tc, _head_mean_matrix()]
        in_specs.append(pl.BlockSpec((1, LANES), lambda i, j: (0, 0)))
        for _ in range(3):
            in_specs.append(pl.BlockSpec((tm, LANES), lambda i, j: (i % period, 0)))
        in_specs.append(pl.BlockSpec((LANES, LANES), lambda i, j: (0, 0)))
    return pl.pallas_call(
        functools.partial(_fmm_kernel, has_norm=has_norm, mode=mode),
        out_shape=jax.ShapeDtypeStruct((m, n), out_dtype),
        grid=(m // tm, n // tn),
        in_specs=in_specs,
        out_specs=pl.BlockSpec((tm, tn), lambda i, j: (i, j)),
        scratch_shapes=[pltpu.VMEM((tm, k), bf16)],
        compiler_params=_cparams(("parallel", "arbitrary")),
        name="fused_matmul_" + mode,
    )(*args)


def _head_mean_matrix():
    r = jnp.arange(LANES) // HEAD_DIM
    return (r[:, None] == r[None, :]).astype(f32) / HEAD_DIM


def _rope_tables(pos):
    half = ROPE_DIM // 2
    inv = ROPE_THETA ** (-jnp.arange(0, ROPE_DIM, 2, dtype=f32) / ROPE_DIM)
    ang = pos.astype(f32)[:, None] * inv[None, :]
    cos, sin = jnp.cos(ang), jnp.sin(ang)
    t = pos.shape[0]
    ones = jnp.ones((t, HEAD_DIM - ROPE_DIM), f32)
    zeros_h = jnp.zeros((t, half), f32)
    zeros_r = jnp.zeros((t, HEAD_DIM - ROPE_DIM), f32)
    ta = jnp.concatenate([cos, cos, ones], axis=1)
    tb = jnp.concatenate([-sin, zeros_h, zeros_r], axis=1)
    tc = jnp.concatenate([zeros_h, sin, zeros_r], axis=1)
    rep = LANES // HEAD_DIM
    return tuple(jnp.tile(a, (1, rep)) for a in (ta, tb, tc))


def _ssd_kernel(z_ref, xbc_ref, dt_ref, h0_ref, cb0_ref, cw_ref, cbias_ref, dtb_ref, alog_ref, dsk_ref, ng_ref,
                ex_ref, y_ref, hout_ref, h_scr, tail_scr, xc_scr, *, chunk, t_valid):
    c = pl.program_id(1)
    nc = pl.num_programs(1)
    L = chunk
    hp = SSM_HEADS_PER_GROUP * SSM_HEAD_DIM

    @pl.when(c == 0)
    def _():
        for g in range(SSM_GROUPS):
            h_scr[g] = h0_ref[g * hp:(g + 1) * hp, :].T
        tail_scr[0:SUBLANES, :] = cb0_ref[...]

    w = cw_ref[...]
    tail_scr[SUBLANES:2 * SUBLANES, :] = xbc_ref[0:SUBLANES, :]
    for lo, hi, src, off in ((0, SUBLANES, tail_scr, SUBLANES), (SUBLANES, L, xbc_ref, 0)):
        acc = src[lo + off:hi + off, :] * w[CONV_W - 1:CONV_W, :] + cbias_ref[...]
        for s in range(1, CONV_W):
            acc = acc + src[lo + off - s:hi + off - s, :] * w[CONV_W - 1 - s:CONV_W - s, :]
        xc_scr[lo:hi, :] = _silu(acc)
    tail_scr[0:SUBLANES, :] = xbc_ref[L - SUBLANES:L, :]

    dt = _softplus(dt_ref[...] + dtb_ref[...])
    if t_valid is not None:
        rowid = c * L + lax.broadcasted_iota(jnp.int32, (L, LANES), 0)
        dt = jnp.where(rowid < t_valid, dt, 0.0)
    a = -jnp.exp(alog_ref[...])
    da = dt * a
    r_i = lax.broadcasted_iota(jnp.int32, (L, L), 0)
    c_i = lax.broadcasted_iota(jnp.int32, (L, L), 1)
    causal = r_i >= c_i
    cum = jnp.dot(causal.astype(f32), da, precision=lax.Precision.HIGHEST, preferred_element_type=f32)
    cum_t = cum.T
    cum_last = cum[L - 1:L, :]
    tailw = jnp.exp(cum_last - cum) * dt
    ecum = jnp.exp(cum)

    src = jnp.concatenate([dt, tailw, ecum], axis=0)
    src_hi = src.astype(bf16)
    src_lo = (src - src_hi.astype(f32)).astype(bf16)
    pieces = jnp.concatenate([src_hi, src_lo], axis=0)

    for g in range(SSM_GROUPS):
        gs = slice(g * hp, (g + 1) * hp)
        ex = jnp.dot(pieces, ex_ref[:, gs], preferred_element_type=f32)
        dt_x = ex[0:L] + ex[3 * L:4 * L]
        tw_x = ex[L:2 * L] + ex[4 * L:5 * L]
        ec_x = ex[2 * L:3 * L] + ex[5 * L:6 * L]
        x_g = xc_scr[:, gs]
        b_g = xc_scr[:, D_INNER + g * D_STATE:D_INNER + (g + 1) * D_STATE].astype(bf16)
        c_g = xc_scr[:, D_INNER + SSM_GROUPS * D_STATE + g * D_STATE:
                     D_INNER + SSM_GROUPS * D_STATE + (g + 1) * D_STATE].astype(bf16)
        cb = _dot_nt(c_g, b_g)
        ht_g = h_scr[g]
        y_in = jnp.dot(c_g, ht_g.astype(bf16), preferred_element_type=f32)
        xdt = (x_g * dt_x).astype(bf16)
        ys = []
        for hh in range(SSM_HEADS_PER_GROUP):
            h = g * SSM_HEADS_PER_GROUP + hh
            seg = cum[:, h:h + 1] - cum_t[h:h + 1, :]
            dec = jnp.exp(jnp.where(causal, seg, NEG_INF))
            m = (cb * dec).astype(bf16)
            ys.append(jnp.dot(m, xdt[:, hh * SSM_HEAD_DIM:(hh + 1) * SSM_HEAD_DIM], preferred_element_type=f32))
        yg = jnp.concatenate(ys, axis=1) + y_in * ec_x + x_g * dsk_ref[:, gs]
        upd = _dot_tn(b_g, (x_g * tw_x).astype(bf16))
        h_scr[g] = ht_g * ec_x[L - 1:L, :] + upd
        yg = yg * _silu(z_ref[:, gs])
        yg = yg * lax.rsqrt(jnp.mean(yg * yg, axis=-1, keepdims=True) + EPS)
        y_ref[:, gs] = (yg * ng_ref[:, gs]).astype(y_ref.dtype)

    @pl.when(c == nc - 1)
    def _():
        for g in range(SSM_GROUPS):
            hout_ref[g * hp:(g + 1) * hp, :] = h_scr[g].T


def ssd_mixer(z, xbc, dtr, h0, cb0, conv_w, conv_b, dt_bias, a_log, d_skip, norm_g, *, t_valid):
    b, t, _ = z.shape
    L = SSD_CHUNK
    assert t % L == 0
    nc = t // L
    hp_all = SSM_HEADS * SSM_HEAD_DIM
    pad_h = LANES - SSM_HEADS

    def lane_pad(v):
        return jnp.pad(v.astype(f32), (0, pad_h)).reshape(1, LANES)

    cw = jnp.pad(conv_w.astype(f32), ((0, SUBLANES - CONV_W), (0, 0)))
    head_of_col = jnp.arange(hp_all) // SSM_HEAD_DIM
    expand = (jnp.arange(LANES)[:, None] == head_of_col[None, :]).astype(bf16)
    dsk_cols = jnp.repeat(d_skip.astype(f32), SSM_HEAD_DIM).reshape(1, hp_all)
    const = lambda shape: pl.BlockSpec(shape, lambda i, j: (0,) * len(shape))
    return pl.pallas_call(
        functools.partial(_ssd_kernel, chunk=L, t_valid=t_valid),
        out_shape=(jax.ShapeDtypeStruct((b, t, D_INNER), bf16),
                   jax.ShapeDtypeStruct((b, hp_all, D_STATE), f32)),
        grid=(b, nc),
        in_specs=[
            pl.BlockSpec((None, L, D_INNER), lambda i, j: (i, j, 0)),
            pl.BlockSpec((None, L, CONV_DIM), lambda i, j: (i, j, 0)),
            pl.BlockSpec((None, L, LANES), lambda i, j: (i, j, 0)),
            pl.BlockSpec((None, hp_all, D_STATE), lambda i, j: (i, 0, 0)),
            pl.BlockSpec((None, SUBLANES, CONV_DIM), lambda i, j: (i, 0, 0)),
            const((SUBLANES, CONV_DIM)),
            const((1, CONV_DIM)),
            const((1, LANES)), const((1, LANES)), const((1, D_INNER)),
            const((1, D_INNER)),
            const((LANES, hp_all)),
        ],
        out_specs=(pl.BlockSpec((None, L, D_INNER), lambda i, j: (i, j, 0)),
                   pl.BlockSpec((None, hp_all, D_STATE), lambda i, j: (i, 0, 0))),
        scratch_shapes=[pltpu.VMEM((SSM_GROUPS, D_STATE, hp_all // SSM_GROUPS), f32),
                        pltpu.VMEM((2 * SUBLANES, CONV_DIM), f32),
                        pltpu.VMEM((L, CONV_DIM), f32)],
        compiler_params=_cparams(("parallel", "arbitrary")),
        name="ssd_mixer",
    )(z, xbc, dtr, h0, cb0, cw, conv_b.reshape(1, CONV_DIM).astype(f32),
      lane_pad(dt_bias), lane_pad(a_log), dsk_cols, norm_g.reshape(1, D_INNER).astype(f32), expand)


def _extract_top(s, n, want_rank):
    vals = []
    rank = jnp.full(s.shape, float(n), f32) if want_rank else None
    for r in range(n):
        m = jnp.max(s, axis=0, keepdims=True)
        vals.append(m)
        hit = s == m
        if want_rank:
            rank = jnp.where(hit, float(r), rank)
        s = jnp.where(hit, NEG_INF, s)
    return vals, rank


def _bf16_pair_words(x):
    hi = pltpu.bitcast(x.astype(bf16).astype(f32), jnp.uint32)
    return hi | (hi >> 16)


def _row_as_bf16_tile(words, rows):
    packed = pltpu.bitcast(jnp.broadcast_to(words, (SUBLANES, words.shape[1])), bf16)
    return jnp.concatenate([packed] * (rows // packed.shape[0]), axis=0)


def _peer_scores_kernel(h_ref, g_ref, wq_ref, sk_ref, xnt_ref, rank_ref, n1_ref, c1_ref, e2_ref, q_scr):
    x = h_ref[...]
    ms = jnp.mean(x * x, axis=-1, keepdims=True)
    xn = x * lax.rsqrt(ms + EPS) * g_ref[...]
    xnt_ref[...] = xn.T.astype(bf16)
    q = jnp.dot(xn.astype(bf16), wq_ref[...], preferred_element_type=f32).astype(bf16)
    for i in range(2 * PEER_HEADS):
        q_scr[i] = q[:, i * PEER_HALF:(i + 1) * PEER_HALF]
    kk = PEER_TOPK
    sk0 = sk_ref[0]
    sk1 = sk_ref[1]

    def head(h, carry):
        s1 = _dot_nt(sk0, q_scr[2 * h])
        s2 = _dot_nt(sk1, q_scr[2 * h + 1])
        av, _ = _extract_top(s1, kk, False)
        bv, rank2 = _extract_top(s2, kk, True)
        cands = []
        for r in range(kk):
            for cc in range(kk // (r + 1)):
                cands.append(av[r] + bv[cc])
        pad = (-len(cands)) % SUBLANES
        cands += [jnp.full_like(cands[0], NEG_INF)] * pad
        best, _ = _extract_top(jnp.concatenate(cands, axis=0), kk, False)
        tau = best[kk - 1]
        top = best[0]
        zsum = jnp.zeros_like(top)
        for r in range(kk):
            zsum = zsum + jnp.exp(best[r] - top)
        n1 = jnp.zeros(s1.shape, f32)
        for cc in range(kk):
            n1 = n1 + jnp.where(s1 + bv[cc] >= tau, 1.0, 0.0)
        rank_ref[h] = rank2.astype(rank_ref.dtype)
        n1_ref[h] = _bf16_pair_words(n1)
        c1_ref[h] = _bf16_pair_words(jnp.exp(s1 - av[0]) / zsum)
        e2_ref[h] = jnp.exp(s2 - bv[0]).astype(e2_ref.dtype)
        return carry

    lax.fori_loop(0, PEER_HEADS, head, 0)


def peer_scores(h, gain, wq, sk, *, tm):
    n, d = h.shape
    nq = wq.shape[1]
    sel16 = jax.ShapeDtypeStruct((PEER_HEADS, PEER_N_KEYS, n), bf16)
    sel32 = jax.ShapeDtypeStruct((PEER_HEADS, PEER_N_KEYS, n), jnp.uint32)
    sel_spec = pl.BlockSpec((PEER_HEADS, PEER_N_KEYS, tm), lambda i: (0, 0, i))
    return pl.pallas_call(
        _peer_scores_kernel,
        out_shape=(jax.ShapeDtypeStruct((d, n), bf16), sel16, sel32, sel32, sel16),
        grid=(n // tm,),
        in_specs=[pl.BlockSpec((tm, d), lambda i: (i, 0)),
                  pl.BlockSpec((1, d), lambda i: (0, 0)),
                  pl.BlockSpec((d, nq), lambda i: (0, 0)),
                  pl.BlockSpec((2, PEER_N_KEYS, PEER_HALF), lambda i: (0, 0, 0))],
        out_specs=(pl.BlockSpec((d, tm), lambda i: (0, i)), sel_spec, sel_spec, sel_spec, sel_spec),
        scratch_shapes=[pltpu.VMEM((2 * PEER_HEADS, tm, PEER_HALF), bf16)],
        compiler_params=_cparams(("parallel",)),
        name="peer_scores",
    )(h, gain.reshape(1, d).astype(f32), wq, sk)


def _peer_dense_kernel(xnt_ref, u_ref, vt_ref, rank_ref, n1_ref, c1_ref, e2_ref, h_ref, o_ref,
                       acc_scr, coef_a, coef_b, *, te, sub, ne):
    j = pl.program_id(1)
    nk = PEER_N_KEYS
    a_per_step = te // nk
    a_per_sub = sub // nk
    tm = xnt_ref.shape[1]
    pk = 2 * SUBLANES
    zero = jnp.zeros((), bf16)

    @pl.when(j == 0)
    def _():
        acc_scr[...] = jnp.zeros_like(acc_scr)
        coef_b[...] = jnp.zeros_like(coef_b)

    def drain(src, part, parts):
        d = acc_scr.shape[0]
        rows = slice(part * d // parts, (part + 1) * d // parts)
        acc_scr[rows, :] += jnp.dot(vt_ref[rows, :], src[...], preferred_element_type=f32)

    def build(dst, src):
        xnt = xnt_ref[...]
        nsub = te // sub

        def scores(s):
            return jnp.dot(u_ref[s * sub:(s + 1) * sub, :], xnt, preferred_element_type=f32)

        act_next = scores(0)
        for s in range(nsub):
            act = act_next
            if s + 1 < nsub:
                act_next = scores(s + 1)
            drain(src, s, nsub)
            for ai in range(a_per_sub):
                a = j * a_per_step + s * a_per_sub + ai
                row0 = (s * a_per_sub + ai) * nk
                lw = tm
                for lc in range(tm // lw):
                    cols = slice(lc * lw, (lc + 1) * lw)
                    wsum = None
                    for h in range(PEER_HEADS):
                        ar = s * a_per_sub + ai
                        n1a = _row_as_bf16_tile(n1_ref[h, j, ar:ar + 1, :][:, cols], nk)
                        c1a = _row_as_bf16_tile(c1_ref[h, j, ar:ar + 1, :][:, cols], nk)
                        term = jnp.where(rank_ref[h, :, cols] < n1a, e2_ref[h, :, cols], zero) * c1a
                        wsum = term if wsum is None else wsum + term
                    g = _gelu_tanh(act[ai * nk:(ai + 1) * nk, cols]).astype(bf16)
                    dst[row0:row0 + nk, cols] = wsum * g

    @pl.when(jnp.logical_and(j < ne, j % 2 == 0))
    def _():
        build(coef_a, coef_b)

    @pl.when(jnp.logical_and(j < ne, j % 2 == 1))
    def _():
        build(coef_b, coef_a)

    @pl.when(j == ne)
    def _():
        drain(coef_b if ne % 2 == 0 else coef_a, 0, 1)
        o_ref[...] = h_ref[...] + acc_scr[...].T


def peer_dense(xnt, u, vt, rank2, n1, c1, e2, h, *, tm, te, sub):
    n, d = h.shape
    n_exp = u.shape[0]
    ne = n_exp // te
    assert te // PEER_N_KEYS == SUBLANES
    sel_spec = pl.BlockSpec((PEER_HEADS, PEER_N_KEYS, tm), lambda i, j: (0, 0, i))
    n1 = n1.reshape(PEER_HEADS, ne, SUBLANES, n)
    c1 = c1.reshape(PEER_HEADS, ne, SUBLANES, n)
    row_spec = pl.BlockSpec((PEER_HEADS, ne, SUBLANES, tm), lambda i, j: (0, 0, 0, i))
    return pl.pallas_call(
        functools.partial(_peer_dense_kernel, te=te, sub=sub, ne=ne),
        out_shape=jax.ShapeDtypeStruct((n, d), f32),
        grid=(n // tm, ne + 1),
        in_specs=[pl.BlockSpec((d, tm), lambda i, j: (0, i)),
                  pl.BlockSpec((te, d), lambda i, j: (jnp.minimum(j, ne - 1), 0)),
                  pl.BlockSpec((d, te), lambda i, j: (0, jnp.maximum(j - 1, 0))),
                  sel_spec, row_spec, row_spec, sel_spec,
                  pl.BlockSpec((tm, d), lambda i, j: (i, 0))],
        out_specs=pl.BlockSpec((tm, d), lambda i, j: (i, 0)),
        scratch_shapes=[pltpu.VMEM((d, tm), f32), pltpu.VMEM((te, tm), bf16), pltpu.VMEM((te, tm), bf16)],
        compiler_params=_cparams(("parallel", "arbitrary")),
        name="peer_dense",
    )(xnt, u, vt, rank2, n1, c1, e2, h)


def _attn_band_kernel(q_ref, kc_ref, kp_ref, vc_ref, vp_ref, o_ref, l_ref):
    i = pl.program_id(2)
    blk = q_ref.shape[0]
    scale = HEAD_DIM ** -0.5
    q = q_ref[...] * scale
    kc, kp, vc, vp = kc_ref[...], kp_ref[...], vc_ref[...], vp_ref[...]
    rows = Q_PER_KV * blk
    a_i = lax.broadcasted_iota(jnp.int32, (rows, blk), 0) % blk
    c_i = lax.broadcasted_iota(jnp.int32, (rows, blk), 1)
    mask_c = c_i <= a_i
    mask_p = jnp.logical_and(c_i >= a_i, i > 0)
    outs, lses = [], []
    for j in range(KV_PER_BRANCH):
        hs = slice(j * HEAD_DIM, (j + 1) * HEAD_DIM)
        qs = jnp.concatenate([q[:, (j * Q_PER_KV + r) * HEAD_DIM:(j * Q_PER_KV + r + 1) * HEAD_DIM]
                              for r in range(Q_PER_KV)], axis=0).astype(bf16)
        s_c = jnp.where(mask_c, _dot_nt(qs, kc[:, hs].astype(bf16)), NEG_INF)
        s_p = jnp.where(mask_p, _dot_nt(qs, kp[:, hs].astype(bf16)), NEG_INF)
        m = jnp.maximum(jnp.max(s_c, axis=-1, keepdims=True), jnp.max(s_p, axis=-1, keepdims=True))
        p_c = jnp.exp(s_c - m)
        p_p = jnp.exp(s_p - m)
        l = jnp.sum(p_c, axis=-1, keepdims=True) + jnp.sum(p_p, axis=-1, keepdims=True)
        o = jnp.dot(p_c.astype(bf16), vc[:, hs].astype(bf16), preferred_element_type=f32)
        o = o + jnp.dot(p_p.astype(bf16), vp[:, hs].astype(bf16), preferred_element_type=f32)
        o = o / l
        lse = jnp.broadcast_to(m + jnp.log(l), (rows, HEAD_DIM))
        for r in range(Q_PER_KV):
            outs.append(o[r * blk:(r + 1) * blk])
            lses.append(lse[r * blk:(r + 1) * blk])
    o_ref[...] = jnp.concatenate(outs, axis=1)
    l_ref[...] = jnp.concatenate(lses, axis=1)


def attn_band(q, k, v, g, dil):
    b, t, _ = q.shape
    s = t // dil
    blk = 128
    assert s % blk == 0
    qc = N_Q_HEADS * HEAD_DIM
    kc = N_KV_HEADS * HEAD_DIM
    hq = Q_PER_KV * KV_PER_BRANCH * HEAD_DIM
    hk = KV_PER_BRANCH * HEAD_DIM
    qv = q.reshape(b, s, dil * qc)
    kv_ = k.reshape(b, s, dil * kc)
    vv = v.reshape(b, s, dil * kc)
    nqb, nkb = qc // hq, kc // hk
    cur = lambda bi, r, i: (bi, i, r * nkb + g)
    prev = lambda bi, r, i: (bi, jnp.maximum(i - 1, 0), r * nkb + g)
    osd = jax.ShapeDtypeStruct((b, s, dil * hq), f32)
    o, l = pl.pallas_call(
        _attn_band_kernel,
        out_shape=(osd, osd),
        grid=(b, dil, s // blk),
        in_specs=[pl.BlockSpec((None, blk, hq), lambda bi, r, i: (bi, i, r * nqb + g)),
                  pl.BlockSpec((None, blk, hk), cur), pl.BlockSpec((None, blk, hk), prev),
                  pl.BlockSpec((None, blk, hk), cur), pl.BlockSpec((None, blk, hk), prev)],
        out_specs=(pl.BlockSpec((None, blk, hq), lambda bi, r, i: (bi, i, r)),
                   pl.BlockSpec((None, blk, hq), lambda bi, r, i: (bi, i, r))),
        compiler_params=_cparams(("parallel", "parallel", "arbitrary")),
        name=f"attn_band_d{dil}",
    )(qv, kv_, kv_, vv, vv)
    return o.reshape(b, t, hq), l.reshape(b, t, hq)


def _merge3_kernel(o0, o1, o2, l0, l1, l2, out_ref):
    a0, a1, a2 = l0[...], l1[...], l2[...]
    m = jnp.maximum(jnp.maximum(a0, a1), a2)
    w0, w1, w2 = jnp.exp(a0 - m), jnp.exp(a1 - m), jnp.exp(a2 - m)
    out_ref[...] = ((w0 * o0[...] + w1 * o1[...] + w2 * o2[...]) / (w0 + w1 + w2)).astype(out_ref.dtype)


def merge3(os_, ls_, *, tm):
    n, c = os_[0].shape
    spec = pl.BlockSpec((tm, c), lambda i: (i, 0))
    return pl.pallas_call(
        _merge3_kernel,
        out_shape=jax.ShapeDtypeStruct((n, c), bf16),
        grid=(n // tm,),
        in_specs=[spec] * 6,
        out_specs=spec,
        compiler_params=_cparams(("parallel",)),
        name="attn_merge",
    )(*os_, *ls_)


def _attn_sample_kernel(q_ref, ck_ref, cv_ref, nk_ref, nv_ref, o_ref, *, t_new, win_buf):
    scale = HEAD_DIM ** -0.5
    q = q_ref[...] * scale
    n_cache = ck_ref.shape[0]
    n_new = nk_ref.shape[0]
    rows = Q_PER_KV * t_new
    t_c = lax.broadcasted_iota(jnp.int32, (rows, n_cache), 0) % t_new
    d_c = win_buf + t_c - lax.broadcasted_iota(jnp.int32, (rows, n_cache), 1)
    t_n = lax.broadcasted_iota(jnp.int32, (rows, n_new), 0) % t_new
    d_n = t_n - lax.broadcasted_iota(jnp.int32, (rows, n_new), 1)
    outs = [None] * (KV_PER_BRANCH * Q_PER_KV)
    for j in range(KV_PER_BRANCH):
        parts = []
        for g, (win, dil) in enumerate(DIL_PATTERNS):
            hq = (g * KV_PER_BRANCH + j) * Q_PER_KV
            hk = slice((g * KV_PER_BRANCH + j) * HEAD_DIM, (g * KV_PER_BRANCH + j + 1) * HEAD_DIM)
            qs = jnp.concatenate([q[:, (hq + r) * HEAD_DIM:(hq + r + 1) * HEAD_DIM] for r in range(Q_PER_KV)],
                                 axis=0).astype(bf16)
            ok_c = (d_c >= 0) & (d_c <= win) & ((d_c & (dil - 1)) == 0)
            ok_n = (d_n >= 0) & (d_n <= win) & ((d_n & (dil - 1)) == 0)
            s_c = jnp.where(ok_c, _dot_nt(qs, ck_ref[:, hk].astype(bf16)), NEG_INF)
            s_n = jnp.where(ok_n, _dot_nt(qs, nk_ref[:, hk].astype(bf16)), NEG_INF)
            parts.append((s_c, s_n, hk))
        m = None
        for s_c, s_n, _ in parts:
            mm = jnp.maximum(jnp.max(s_c, axis=-1, keepdims=True), jnp.max(s_n, axis=-1, keepdims=True))
            m = mm if m is None else jnp.maximum(m, mm)
        l = jnp.zeros_like(m)
        o = jnp.zeros((rows, HEAD_DIM), f32)
        for s_c, s_n, hk in parts:
            p_c = jnp.exp(s_c - m)
            p_n = jnp.exp(s_n - m)
            l = l + jnp.sum(p_c, axis=-1, keepdims=True) + jnp.sum(p_n, axis=-1, keepdims=True)
            o = o + jnp.dot(p_c.astype(bf16), cv_ref[:, hk].astype(bf16), preferred_element_type=f32)
            o = o + jnp.dot(p_n.astype(bf16), nv_ref[:, hk].astype(bf16), preferred_element_type=f32)
        o = o / l
        for r in range(Q_PER_KV):
            outs[j * Q_PER_KV + r] = o[r * t_new:(r + 1) * t_new]
    o_ref[...] = jnp.concatenate(outs, axis=1).astype(o_ref.dtype)


def attn_sample(q, ck, cv, nk, nv):
    b, t_new, qc = q.shape
    w = ck.shape[1]
    kc = ck.shape[2]
    n_new = nk.shape[1]
    return pl.pallas_call(
        functools.partial(_attn_sample_kernel, t_new=t_new, win_buf=w),
        out_shape=jax.ShapeDtypeStruct((b, t_new, ATTN_OUT_DIM), bf16),
        grid=(b,),
        in_specs=[pl.BlockSpec((None, t_new, qc), lambda i: (i, 0, 0)),
                  pl.BlockSpec((None, w, kc), lambda i: (i, 0, 0)),
                  pl.BlockSpec((None, w, kc), lambda i: (i, 0, 0)),
                  pl.BlockSpec((None, n_new, kc), lambda i: (i, 0, 0)),
                  pl.BlockSpec((None, n_new, kc), lambda i: (i, 0, 0))],
        out_specs=pl.BlockSpec((None, t_new, ATTN_OUT_DIM), lambda i: (i, 0, 0)),
        compiler_params=_cparams(("parallel",)),
        name="attn_sample",
    )(q, ck, cv, nk, nv)


def _prep_weights(norm_mix, norm_ffn, norm_ple, ssm_w_in, ssm_conv_w, ssm_conv_b, ssm_dt_bias, ssm_a_log, ssm_d,
                  ssm_norm, ssm_w_out, kv_norm, w_kv, k_norm, w_q, q_norm, w_o,
                  peer_w_query, peer_sub_keys, peer_u, peer_v, ple_w_proj, ple_w_gate):
    zx = D_INNER + CONV_DIM
    w_in = ssm_w_in[0]
    kdim = N_KV_HEADS * HEAD_DIM
    rep = LANES // HEAD_DIM
    return dict(
        norm_mix=norm_mix, norm_ffn=norm_ffn, norm_ple=norm_ple,
        w_z=w_in[:, :D_INNER].astype(bf16), w_xbc=w_in[:, D_INNER:zx].astype(bf16),
        w_dt=jnp.pad(w_in[:, zx:], ((0, 0), (0, LANES - SSM_HEADS))).astype(bf16),
        conv_w=ssm_conv_w[0], conv_b=ssm_conv_b[0], dt_bias=ssm_dt_bias[0], a_log=ssm_a_log[0],
        d_skip=ssm_d[0], ssm_norm=ssm_norm[0], w_out=ssm_w_out[0].astype(bf16),
        kv_norm=kv_norm, w_k=w_kv[:, :kdim].astype(bf16), w_v=w_kv[:, kdim:].astype(bf16),
        k_gain=jnp.tile(k_norm.astype(f32), rep).reshape(1, LANES),
        w_q=w_q[0].astype(bf16), q_gain=jnp.tile(q_norm[0].astype(f32), rep).reshape(1, LANES),
        w_o=w_o[0].astype(bf16),
        peer_wq=[peer_w_query[i].astype(bf16) for i in range(DEPTH)],
        peer_sk=[peer_sub_keys[i].astype(bf16) for i in range(DEPTH)],
        peer_u=[peer_u[i].astype(bf16) for i in range(DEPTH)],
        peer_vt=[peer_v[i].astype(bf16).T for i in range(DEPTH)],
        ple_wp=[ple_w_proj[i].astype(bf16) for i in range(DEPTH)],
        ple_wg=[ple_w_gate[i].astype(bf16) for i in range(DEPTH)],
    )


PEER_TOKEN_TILE = 512
PEER_SCORE_TILE = 256
PEER_EXPERT_TILE = SUBLANES * PEER_N_KEYS
PEER_SUB_TILE = 256


def _peer_ple(h, p, i, wts, tm):
    n = h.shape[0]
    tm_d = PEER_TOKEN_TILE if n % PEER_TOKEN_TILE == 0 else n
    xnt, rank2, n1, c1, e2 = peer_scores(h, wts["norm_ffn"][i], wts["peer_wq"][i], wts["peer_sk"][i],
                                         tm=PEER_SCORE_TILE)
    h = peer_dense(xnt, wts["peer_u"][i], wts["peer_vt"][i], rank2, n1, c1, e2, h,
                   tm=tm_d, te=PEER_EXPERT_TILE, sub=PEER_SUB_TILE)
    return fused_matmul(h, wts["ple_wg"][i], gain=wts["norm_ple"][i], mode="ple", res=h,
                        ple=(p, wts["ple_wp"][i]), tm=tm)


def _trunk(x, p, conv_state, ssm_state, past_k, past_v, pos0, wts):
    b, t, d = x.shape
    n = b * t
    prompt = past_k is None
    tm = _token_tile(n)
    h = x.reshape(n, d)

    z = fused_matmul(h, wts["w_z"], gain=wts["norm_mix"][0], tm=tm).reshape(b, t, D_INNER)
    xbc = fused_matmul(h, wts["w_xbc"], gain=wts["norm_mix"][0], tm=tm).reshape(b, t, CONV_DIM)
    dtr = fused_matmul(h, wts["w_dt"], gain=wts["norm_mix"][0], tm=tm)
    new_conv = xbc[:, t - (CONV_W - 1):, :] if t >= CONV_W - 1 else None
    dtr = dtr.reshape(b, t, LANES)
    tp = -(-t // SSD_CHUNK) * SSD_CHUNK
    if tp != t:
        padt = ((0, 0), (0, tp - t), (0, 0))
        z_p, xbc_p, dtr_p = jnp.pad(z, padt), jnp.pad(xbc, padt), jnp.pad(dtr, padt)
    else:
        z_p, xbc_p, dtr_p = z, xbc, dtr
    cb0 = jnp.pad(conv_state.astype(f32), ((0, 0), (SUBLANES - (CONV_W - 1), 0), (0, 0)))
    h0 = ssm_state.astype(f32).reshape(b, SSM_HEADS * SSM_HEAD_DIM, D_STATE)
    y, h_fin = ssd_mixer(z_p, xbc_p, dtr_p, h0, cb0, wts["conv_w"], wts["conv_b"], wts["dt_bias"], wts["a_log"],
                         wts["d_skip"], wts["ssm_norm"], t_valid=(None if tp == t else t))
    y = y[:, :t].reshape(n, D_INNER)
    h = fused_matmul(y, wts["w_out"], mode="add", res=h, tm=tm)
    new_ssm = h_fin.reshape(b, SSM_HEADS, SSM_HEAD_DIM, D_STATE)
    h = _peer_ple(h, p[0].reshape(n, -1), 0, wts, tm)

    pos = pos0 + jnp.arange(t, dtype=jnp.int32)
    tabs = _rope_tables(pos)
    if t < tm:
        tabs = tuple(jnp.tile(a, (tm // t, 1)) for a in tabs)
    k_new = fused_matmul(h, wts["w_k"], gain=wts["kv_norm"], mode="headrope",
                         headrope=(wts["k_gain"],) + tabs, tm=tm)
    v_new = fused_matmul(h, wts["w_v"], gain=wts["kv_norm"], tm=tm)

    q = fused_matmul(h, wts["w_q"], gain=wts["norm_mix"][1], mode="headrope",
                     headrope=(wts["q_gain"],) + tabs, tm=tm)
    kdim = N_KV_HEADS * HEAD_DIM
    if prompt:
        q3, k3, v3 = q.reshape(b, t, -1), k_new.reshape(b, t, kdim), v_new.reshape(b, t, kdim)
        os_, ls_ = [], []
        for g, (_, dil) in enumerate(DIL_PATTERNS):
            o_g, l_g = attn_band(q3, k3, v3, g, dil)
            os_.append(o_g.reshape(n, ATTN_OUT_DIM))
            ls_.append(l_g.reshape(n, ATTN_OUT_DIM))
        att = merge3(os_, ls_, tm=tm)
    else:
        padn = ((0, 0), (0, SSD_CHUNK - t), (0, 0))
        nk = jnp.pad(k_new.reshape(b, t, kdim), padn)
        nv = jnp.pad(v_new.reshape(b, t, kdim), padn)
        w = past_k.shape[1]
        att = attn_sample(q.reshape(b, t, -1), past_k.reshape(b, w, kdim).astype(f32),
                          past_v.reshape(b, w, kdim).astype(f32), nk, nv).reshape(n, ATTN_OUT_DIM)
    h = fused_matmul(att, wts["w_o"], mode="add", res=h, tm=tm)
    h = _peer_ple(h, p[1].reshape(n, -1), 1, wts, tm)

    return (h.reshape(b, t, d), new_conv[None], new_ssm[None],
            k_new.reshape(b, t, N_KV_HEADS, HEAD_DIM), v_new.reshape(b, t, N_KV_HEADS, HEAD_DIM))


def kernel(x_prompt, x_sample, state_conv, state_ssm, cache_k, cache_v, p_prompt, p_sample, norm_mix, norm_ffn, norm_ple, ssm_w_in, ssm_conv_w, ssm_conv_b, ssm_dt_bias, ssm_a_log, ssm_d, ssm_norm, ssm_w_out, kv_norm, w_kv, k_norm, w_q, q_norm, w_o, peer_w_query, peer_sub_keys, peer_u, peer_v, ple_w_proj, ple_w_gate):
    wts = _prep_weights(norm_mix, norm_ffn, norm_ple, ssm_w_in, ssm_conv_w, ssm_conv_b, ssm_dt_bias, ssm_a_log,
                        ssm_d, ssm_norm, ssm_w_out, kv_norm, w_kv, k_norm, w_q, q_norm, w_o,
                        peer_w_query, peer_sub_keys, peer_u, peer_v, ple_w_proj, ple_w_gate)
    b_p, t_p, _ = x_prompt.shape
    zero_conv = jnp.zeros((b_p, CONV_W - 1, CONV_DIM), x_prompt.dtype)
    zero_ssm = jnp.zeros((b_p, SSM_HEADS, SSM_HEAD_DIM, D_STATE), x_prompt.dtype)
    y_p, conv_p, ssm_p, k_p, v_p = _trunk(x_prompt, p_prompt, zero_conv, zero_ssm, None, None, 0, wts)
    keep = min(max(w for w, _ in DIL_PATTERNS), t_p)
    past_len = 16384
    y_s, conv_s, ssm_s, k_s, v_s = _trunk(x_sample, p_sample, state_conv[0], state_ssm[0], cache_k, cache_v,
                                          past_len, wts)
    return (y_p, y_s, conv_p, ssm_p, k_p[:, -keep:], v_p[:, -keep:], conv_s, ssm_s, k_s, v_s)
```

```python
import functools
import math

import jax
import jax.numpy as jnp
from jax import lax
from jax.experimental import pallas as pl
from jax.experimental.pallas import tpu as pltpu

f32 = jnp.float32
bf16 = jnp.bfloat16

D_MODEL = 1024
DEPTH = 2
N_A_LAYERS = 1
D_INNER = 2048
SSM_HEAD_DIM = 64
SSM_HEADS = 32
SSM_GROUPS = 8
SSM_HEADS_PER_GROUP = 4
D_STATE = 128
CONV_W = 4
CONV_DIM = 4096
SSD_CHUNK = 128
HEAD_DIM = 64
DIL_PATTERNS = ((128, 1), (512, 4), (2048, 16))
N_DIL = 3
KV_PER_BRANCH = 2
Q_PER_KV = 4
N_Q_HEADS = 24
N_KV_HEADS = 6
ATTN_OUT_DIM = 512
ROPE_DIM = 16
ROPE_THETA = 500000.0
PEER_HEADS = 8
PEER_N_KEYS = 128
PEER_TOPK = 16
PEER_HALF = 128
EPS = 1e-6

LANES = 128
SUBLANES = 8
VMEM_LIMIT_BYTES = 56 * 1024 * 1024

NEG_INF = float("-inf")


def _cparams(sem):
    return pltpu.CompilerParams(dimension_semantics=sem, vmem_limit_bytes=VMEM_LIMIT_BYTES)


def _dot_nt(a, b):
    return lax.dot_general(a, b, (((1,), (1,)), ((), ())), preferred_element_type=f32)


def _dot_tn(a, b):
    return lax.dot_general(a, b, (((0,), (0,)), ((), ())), preferred_element_type=f32)


def _sigmoid(x):
    return 1.0 / (1.0 + jnp.exp(-x))


def _silu(x):
    return x * _sigmoid(x)


def _softplus(x):
    return jnp.maximum(x, 0.0) + jnp.log1p(jnp.exp(-jnp.abs(x)))


def _gelu_tanh(x):
    c = math.sqrt(2.0 / math.pi)
    return x * (0.5 + 0.5 * jnp.tanh(x * (c + (c * 0.044715) * (x * x))))


def _fmm_kernel(*refs, has_norm, mode):
    it = iter(refs)
    x_ref = next(it)
    g_ref = next(it) if has_norm else None
    w_ref = next(it)
    res_ref = next(it) if mode in ("add", "ple") else None
    if mode == "ple":
        p_ref, wp_ref = next(it), next(it)
    if mode == "headrope":
        hg_ref, ta_ref, tb_ref, tc_ref, bd_ref = next(it), next(it), next(it), next(it), next(it)
    o_ref = next(it)
    xn_ref = next(it)

    @pl.when(pl.program_id(1) == 0)
    def _():
        x = x_ref[...].astype(f32)
        if has_norm:
            ms = jnp.mean(x * x, axis=-1, keepdims=True)
            x = x * lax.rsqrt(ms + EPS) * g_ref[...]
        xn_ref[...] = x.astype(bf16)

    acc = jnp.dot(xn_ref[...], w_ref[...], preferred_element_type=f32)
    if mode == "plain":
        o_ref[...] = acc.astype(o_ref.dtype)
    elif mode == "add":
        o_ref[...] = res_ref[...] + acc
    elif mode == "ple":
        pp = jnp.dot(p_ref[...].astype(bf16), wp_ref[...], preferred_element_type=f32)
        o_ref[...] = res_ref[...] + pp * _sigmoid(acc)
    elif mode == "headrope":
        tn = acc.shape[1]
        ta, tb, tc = ta_ref[...], tb_ref[...], tc_ref[...]
        hg = hg_ref[...]
        bd = bd_ref[...]
        for nb in range(tn // LANES):
            blk = acc[:, nb * LANES:(nb + 1) * LANES]
            ms = jnp.dot(blk * blk, bd, precision=lax.Precision.HIGHEST, preferred_element_type=f32)
            y = blk * lax.rsqrt(ms + EPS) * hg
            y = y * ta + pltpu.roll(y, LANES - ROPE_DIM // 2, 1) * tb + pltpu.roll(y, ROPE_DIM // 2, 1) * tc
            o_ref[:, nb * LANES:(nb + 1) * LANES] = y


MATMUL_VMEM_BUDGET = 40 * 1024 * 1024


def _token_tile(m):
    return next((c for c in (1024, 512, 256, 128) if m % c == 0), m)


def _col_tile(tm, k, n, x_bytes, mode, kp):
    out_blocks = 2 if mode in ("add", "ple") else 1
    for tn in sorted({n, 2048, 1536, 1024, 768, 512, 384, 256, 128}, reverse=True):
        if tn > n or n % tn:
            continue
        use = 2 * tm * k * x_bytes + tm * k * 2 + 2 * k * tn * 2 + 2 * out_blocks * tm * tn * 4
        if mode == "ple":
            use += 2 * tm * kp * 4 + 2 * kp * tn * 2
        if use <= MATMUL_VMEM_BUDGET:
            return tn
    raise ValueError("no column tile fits")


def fused_matmul(x, w, *, gain=None, mode="plain", res=None, ple=None, headrope=None, tm, tn=None, out_dtype=f32):
    m, k = x.shape
    n = w.shape[1]
    if tn is None:
        tn = _col_tile(tm, k, n, x.dtype.itemsize, mode, ple[0].shape[1] if ple else 0)
    assert m % tm == 0 and n % tn == 0, (m, tm, n, tn)
    has_norm = gain is not None
    args = [x]
    in_specs = [pl.BlockSpec((tm, k), lambda i, j: (i, 0))]
    if has_norm:
        args.append(gain.reshape(1, k).astype(f32))
        in_specs.append(pl.BlockSpec((1, k), lambda i, j: (0, 0)))
    args.append(w)
    in_specs.append(pl.BlockSpec((k, tn), lambda i, j: (0, j)))
    if mode in ("add", "ple"):
        args.append(res)
        in_specs.append(pl.BlockSpec((tm, tn), lambda i, j: (i, j)))
    if mode == "ple":
        p, wp = ple
        kp = p.shape[1]
        args += [p, wp]
        in_specs += [pl.BlockSpec((tm, kp), lambda i, j: (i, 0)), pl.BlockSpec((kp, tn), lambda i, j: (0, j))]
    if mode == "headrope":
        hg, ta, tb, tc = headrope
        period = ta.shape[0] // tm
        args += [hg, ta, tb, tc, _head_mean_matrix()]
        in_specs.append(pl.BlockSpec((1, LANES), lambda i, j: (0, 0)))
        for _ in range(3):
            in_specs.append(pl.BlockSpec((tm, LANES), lambda i, j: (i % period, 0)))
        in_specs.append(pl.BlockSpec((LANES, LANES), lambda i, j: (0, 0)))
    return pl.pallas_call(
        functools.partial(_fmm_kernel, has_norm=has_norm, mode=mode),
        out_shape=jax.ShapeDtypeStruct((m, n), out_dtype),
        grid=(m // tm, n // tn),
        in_specs=in_specs,
        out_specs=pl.BlockSpec((tm, tn), lambda i, j: (i, j)),
        scratch_shapes=[pltpu.VMEM((tm, k), bf16)],
        compiler_params=_cparams(("parallel", "arbitrary")),
        name="fused_matmul_" + mode,
    )(*args)


def _head_mean_matrix():
    r = jnp.arange(LANES) // HEAD_DIM
    return (r[:, None] == r[None, :]).astype(f32) / HEAD_DIM


def _rope_tables(pos):
    half = ROPE_DIM // 2
    inv = ROPE_THETA ** (-jnp.arange(0, ROPE_DIM, 2, dtype=f32) / ROPE_DIM)
    ang = pos.astype(f32)[:, None] * inv[None, :]
    cos, sin = jnp.cos(ang), jnp.sin(ang)
    t = pos.shape[0]
    ones = jnp.ones((t, HEAD_DIM - ROPE_DIM), f32)
    zeros_h = jnp.zeros((t, half), f32)
    zeros_r = jnp.zeros((t, HEAD_DIM - ROPE_DIM), f32)
    ta = jnp.concatenate([cos, cos, ones], axis=1)
    tb = jnp.concatenate([-sin, zeros_h, zeros_r], axis=1)
    tc = jnp.concatenate([zeros_h, sin, zeros_r], axis=1)
    rep = LANES // HEAD_DIM
    return tuple(jnp.tile(a, (1, rep)) for a in (ta, tb, tc))


def _ssd_kernel(z_ref, xbc_ref, dt_ref, h0_ref, cb0_ref, cw_ref, cbias_ref, dtb_ref, alog_ref, dsk_ref, ng_ref,
                ex_ref, y_ref, hout_ref, h_scr, tail_scr, xc_scr, *, chunk, t_valid):
    c = pl.program_id(1)
    nc = pl.num_programs(1)
    L = chunk
    hp = SSM_HEADS_PER_GROUP * SSM_HEAD_DIM

    @pl.when(c == 0)
    def _():
        for g in range(SSM_GROUPS):
            h_scr[g] = h0_ref[g * hp:(g + 1) * hp, :].T
        tail_scr[0:SUBLANES, :] = cb0_ref[...]

    w = cw_ref[...]
    tail_scr[SUBLANES:2 * SUBLANES, :] = xbc_ref[0:SUBLANES, :]
    for lo, hi, src, off in ((0, SUBLANES, tail_scr, SUBLANES), (SUBLANES, L, xbc_ref, 0)):
        acc = src[lo + off:hi + off, :] * w[CONV_W - 1:CONV_W, :] + cbias_ref[...]
        for s in range(1, CONV_W):
            acc = acc + src[lo + off - s:hi + off - s, :] * w[CONV_W - 1 - s:CONV_W - s, :]
        xc_scr[lo:hi, :] = _silu(acc)
    tail_scr[0:SUBLANES, :] = xbc_ref[L - SUBLANES:L, :]

    dt = _softplus(dt_ref[...] + dtb_ref[...])
    if t_valid is not None:
        rowid = c * L + lax.broadcasted_iota(jnp.int32, (L, LANES), 0)
        dt = jnp.where(rowid < t_valid, dt, 0.0)
    a = -jnp.exp(alog_ref[...])
    da = dt * a
    r_i = lax.broadcasted_iota(jnp.int32, (L, L), 0)
    c_i = lax.broadcasted_iota(jnp.int32, (L, L), 1)
    causal = r_i >= c_i
    cum = jnp.dot(causal.astype(f32), da, precision=lax.Precision.HIGHEST, preferred_element_type=f32)
    cum_t = cum.T
    cum_last = cum[L - 1:L, :]
    tailw = jnp.exp(cum_last - cum) * dt
    ecum = jnp.exp(cum)

    src = jnp.concatenate([dt, tailw, ecum], axis=0)
    src_hi = src.astype(bf16)
    src_lo = (src - src_hi.astype(f32)).astype(bf16)
    pieces = jnp.concatenate([src_hi, src_lo], axis=0)

    for g in range(SSM_GROUPS):
        gs = slice(g * hp, (g + 1) * hp)
        ex = jnp.dot(pieces, ex_ref[:, gs], preferred_element_type=f32)
        dt_x = ex[0:L] + ex[3 * L:4 * L]
        tw_x = ex[L:2 * L] + ex[4 * L:5 * L]
        ec_x = ex[2 * L:3 * L] + ex[5 * L:6 * L]
        x_g = xc_scr[:, gs]
        b_g = xc_scr[:, D_INNER + g * D_STATE:D_INNER + (g + 1) * D_STATE].astype(bf16)
        c_g = xc_scr[:, D_INNER + SSM_GROUPS * D_STATE + g * D_STATE:
                     D_INNER + SSM_GROUPS * D_STATE + (g + 1) * D_STATE].astype(bf16)
        cb = _dot_nt(c_g, b_g)
        ht_g = h_scr[g]
        y_in = jnp.dot(c_g, ht_g.astype(bf16), preferred_element_type=f32)
        xdt = (x_g * dt_x).astype(bf16)
        ys = []
        for hh in range(SSM_HEADS_PER_GROUP):
            h = g * SSM_HEADS_PER_GROUP + hh
            seg = cum[:, h:h + 1] - cum_t[h:h + 1, :]
            dec = jnp.exp(jnp.where(causal, seg, NEG_INF))
            m = (cb * dec).astype(bf16)
            ys.append(jnp.dot(m, xdt[:, hh * SSM_HEAD_DIM:(hh + 1) * SSM_HEAD_DIM], preferred_element_type=f32))
        yg = jnp.concatenate(ys, axis=1) + y_in * ec_x + x_g * dsk_ref[:, gs]
        upd = _dot_tn(b_g, (x_g * tw_x).astype(bf16))
        h_scr[g] = ht_g * ec_x[L - 1:L, :] + upd
        yg = yg * _silu(z_ref[:, gs])
        yg = yg * lax.rsqrt(jnp.mean(yg * yg, axis=-1, keepdims=True) + EPS)
        y_ref[:, gs] = (yg * ng_ref[:, gs]).astype(y_ref.dtype)

    @pl.when(c == nc - 1)
    def _():
        for g in range(SSM_GROUPS):
            hout_ref[g * hp:(g + 1) * hp, :] = h_scr[g].T


def ssd_mixer(z, xbc, dtr, h0, cb0, conv_w, conv_b, dt_bias, a_log, d_skip, norm_g, *, t_valid):
    b, t, _ = z.shape
    L = SSD_CHUNK
    assert t % L == 0
    nc = t // L
    hp_all = SSM_HEADS * SSM_HEAD_DIM
    pad_h = LANES - SSM_HEADS

    def lane_pad(v):
        return jnp.pad(v.astype(f32), (0, pad_h)).reshape(1, LANES)

    cw = jnp.pad(conv_w.astype(f32), ((0, SUBLANES - CONV_W), (0, 0)))
    head_of_col = jnp.arange(hp_all) // SSM_HEAD_DIM
    expand = (jnp.arange(LANES)[:, None] == head_of_col[None, :]).astype(bf16)
    dsk_cols = jnp.repeat(d_skip.astype(f32), SSM_HEAD_DIM).reshape(1, hp_all)
    const = lambda shape: pl.BlockSpec(shape, lambda i, j: (0,) * len(shape))
    return pl.pallas_call(
        functools.partial(_ssd_kernel, chunk=L, t_valid=t_valid),
        out_shape=(jax.ShapeDtypeStruct((b, t, D_INNER), bf16),
                   jax.ShapeDtypeStruct((b, hp_all, D_STATE), f32)),
        grid=(b, nc),
        in_specs=[
            pl.BlockSpec((None, L, D_INNER), lambda i, j: (i, j, 0)),
            pl.BlockSpec((None, L, CONV_DIM), lambda i, j: (i, j, 0)),
            pl.BlockSpec((None, L, LANES), lambda i, j: (i, j, 0)),
            pl.BlockSpec((None, hp_all, D_STATE), lambda i, j: (i, 0, 0)),
            pl.BlockSpec((None, SUBLANES, CONV_DIM), lambda i, j: (i, 0, 0)),
            const((SUBLANES, CONV_DIM)),
            const((1, CONV_DIM)),
            const((1, LANES)), const((1, LANES)), const((1, D_INNER)),
            const((1, D_INNER)),
            const((LANES, hp_all)),
        ],
        out_specs=(pl.BlockSpec((None, L, D_INNER), lambda i, j: (i, j, 0)),
                   pl.BlockSpec((None, hp_all, D_STATE), lambda i, j: (i, 0, 0))),
        scratch_shapes=[pltpu.VMEM((SSM_GROUPS, D_STATE, hp_all // SSM_GROUPS), f32),
                        pltpu.VMEM((2 * SUBLANES, CONV_DIM), f32),
                        pltpu.VMEM((L, CONV_DIM), f32)],
        compiler_params=_cparams(("parallel", "arbitrary")),
        name="ssd_mixer",
    )(z, xbc, dtr, h0, cb0, cw, conv_b.reshape(1, CONV_DIM).astype(f32),
      lane_pad(dt_bias), lane_pad(a_log), dsk_cols, norm_g.reshape(1, D_INNER).astype(f32), expand)


def _sort_network(n):
    def merge(lo, hi, r):
        step = r * 2
        if step < hi - lo:
            yield from merge(lo, hi, step)
            yield from merge(lo + r, hi, step)
            yield from [(i, i + r) for i in range(lo + r, hi - r, step)]
        else:
            yield (lo, lo + r)

    def sort(lo, hi):
        if hi - lo >= 1:
            mid = lo + (hi - lo) // 2
            yield from sort(lo, mid)
            yield from sort(mid + 1, hi)
            yield from merge(lo, hi, 1)

    return list(sort(0, n - 1))


def _sublane_max(x):
    for shift in (4, 2, 1):
        x = jnp.maximum(x, pltpu.roll(x, shift, 0))
    return x


def _pop_columns(cols, extra, n):
    cols = list(cols)
    out = []
    for r in range(n):
        top = cols[0] if extra is None else jnp.maximum(cols[0], extra)
        m = _sublane_max(top)
        out.append(m)
        if r + 1 == n:
            break
        hit = cols[0] == m
        for i in range(min(n - r - 1, len(cols))):
            below = cols[i + 1] if i + 1 < len(cols) else NEG_INF
            cols[i] = jnp.where(hit, below, cols[i])
        if extra is not None:
            extra = jnp.where(extra == m, NEG_INF, extra)
    return out


def _top_values(s, n):
    k = s.shape[0] // SUBLANES
    v = [s[i * SUBLANES:(i + 1) * SUBLANES] for i in range(k)]
    for i, j in _sort_network(k):
        v[i], v[j] = jnp.maximum(v[i], v[j]), jnp.minimum(v[i], v[j])
    return _pop_columns(v[:n], None, n)


def _bf16_pair_words(x):
    hi = pltpu.bitcast(x.astype(bf16).astype(f32), jnp.uint32)
    return hi | (hi >> 16)


def _row_as_bf16_tile(words, rows):
    packed = pltpu.bitcast(jnp.broadcast_to(words, (SUBLANES, words.shape[1])), bf16)
    return jnp.concatenate([packed] * (rows // packed.shape[0]), axis=0)


def _peer_scores_kernel(h_ref, g_ref, wq_ref, sk_ref, xnt_ref, rank_ref, n1_ref, c1_ref, e2_ref, q_scr):
    x = h_ref[...]
    ms = jnp.mean(x * x, axis=-1, keepdims=True)
    xn = x * lax.rsqrt(ms + EPS) * g_ref[...]
    xnt_ref[...] = xn.T.astype(bf16)
    q = jnp.dot(xn.astype(bf16), wq_ref[...], preferred_element_type=f32).astype(bf16)
    for i in range(2 * PEER_HEADS):
        q_scr[i] = q[:, i * PEER_HALF:(i + 1) * PEER_HALF]
    kk = PEER_TOPK
    sk0 = sk_ref[0]
    sk1 = sk_ref[1]

    nkeys = sk0.shape[0]
    tokens = h_ref.shape[0]

    def rep(tile):
        return jnp.concatenate([tile] * (nkeys // SUBLANES), axis=0)

    def select_half(s1, s2):
        av = _top_values(s1, kk)
        bv = _top_values(s2, kk)
        sub = lax.broadcasted_iota(jnp.int32, av[0].shape, 0)
        a_lo, a_hi = av[SUBLANES - 1], av[2 * SUBLANES - 1]
        for r in range(SUBLANES - 2, -1, -1):
            a_lo = jnp.where(sub == r, av[r], a_lo)
            a_hi = jnp.where(sub == r, av[SUBLANES + r], a_hi)
        cand = [a_lo + bv[0]]
        for c in range(1, kk):
            cand.append(jnp.where(sub < kk // (c + 1), a_lo + bv[c], NEG_INF))
        best = _pop_columns(cand, a_hi + bv[0], kk)
        tau, top = best[kk - 1], best[0]
        zsum = jnp.zeros_like(top)
        for r in range(kk):
            zsum = zsum + jnp.exp(best[r] - top)
        tau_r = rep(tau)
        rank2 = jnp.full(s2.shape, float(kk), f32)
        n1 = jnp.zeros(s1.shape, f32)
        for c in range(kk - 1, -1, -1):
            b_c = rep(bv[c])
            rank2 = jnp.where(s2 >= b_c, float(c), rank2)
            n1 = n1 + jnp.where(s1 + b_c >= tau_r, 1.0, 0.0)
        c1 = jnp.exp(s1 - rep(av[0])) / rep(zsum)
        e2 = jnp.exp(s2 - rep(bv[0]))
        return rank2, n1, c1, e2

    def head(h, carry):
        s1 = _dot_nt(sk0, q_scr[2 * h])
        s2 = _dot_nt(sk1, q_scr[2 * h + 1])
        parts = [select_half(s1[:, l0:l0 + LANES], s2[:, l0:l0 + LANES]) for l0 in range(0, tokens, LANES)]
        rank2, n1, c1, e2 = (jnp.concatenate(p, axis=1) for p in zip(*parts))
        rank_ref[h] = rank2.astype(rank_ref.dtype)
        n1_ref[h] = _bf16_pair_words(n1)
        c1_ref[h] = _bf16_pair_words(c1)
        e2_ref[h] = e2.astype(e2_ref.dtype)
        return carry

    lax.fori_loop(0, PEER_HEADS, head, 0)


def peer_scores(h, gain, wq, sk, *, tm):
    n, d = h.shape
    nq = wq.shape[1]
    sel16 = jax.ShapeDtypeStruct((PEER_HEADS, PEER_N_KEYS, n), bf16)
    sel32 = jax.ShapeDtypeStruct((PEER_HEADS, PEER_N_KEYS, n), jnp.uint32)
    sel_spec = pl.BlockSpec((PEER_HEADS, PEER_N_KEYS, tm), lambda i: (0, 0, i))
    return pl.pallas_call(
        _peer_scores_kernel,
        out_shape=(jax.ShapeDtypeStruct((d, n), bf16), sel16, sel32, sel32, sel16),
        grid=(n // tm,),
        in_specs=[pl.BlockSpec((tm, d), lambda i: (i, 0)),
                  pl.BlockSpec((1, d), lambda i: (0, 0)),
                  pl.BlockSpec((d, nq), lambda i: (0, 0)),
                  pl.BlockSpec((2, PEER_N_KEYS, PEER_HALF), lambda i: (0, 0, 0))],
        out_specs=(pl.BlockSpec((d, tm), lambda i: (0, i)), sel_spec, sel_spec, sel_spec, sel_spec),
        scratch_shapes=[pltpu.VMEM((2 * PEER_HEADS, tm, PEER_HALF), bf16)],
        compiler_params=_cparams(("parallel",)),
        name="peer_scores",
    )(h, gain.reshape(1, d).astype(f32), wq, sk)


def _peer_dense_kernel(xnt_ref, u_ref, vt_ref, rank_ref, n1_ref, c1_ref, e2_ref, h_ref, o_ref,
                       acc_scr, coef_a, coef_b, *, te, sub, ne, vsplit):
    j = pl.program_id(1)
    nk = PEER_N_KEYS
    a_per_step = te // nk
    a_per_sub = sub // nk
    tm = xnt_ref.shape[1]
    pk = 2 * SUBLANES
    zero = jnp.zeros((), bf16)

    @pl.when(j == 0)
    def _():
        acc_scr[...] = jnp.zeros_like(acc_scr)
        coef_b[...] = jnp.zeros_like(coef_b)

    def drain(src, part, parts):
        d = acc_scr.shape[0]
        rows = slice(part * d // parts, (part + 1) * d // parts)
        acc_scr[rows, :] += jnp.dot(vt_ref[rows, :], src[...], preferred_element_type=f32)

    def build(dst, src):
        xnt = xnt_ref[...]
        nsub = te // sub

        def scores(s):
            return jnp.dot(u_ref[s * sub:(s + 1) * sub, :], xnt, preferred_element_type=f32)

        def masks(act, s, lc, lw):
            cols = slice(lc * lw, (lc + 1) * lw)
            for ai in range(a_per_sub):
                ar = s * a_per_sub + ai
                wsum = None
                for h in range(PEER_HEADS):
                    n1a = _row_as_bf16_tile(n1_ref[h, j, ar:ar + 1, :][:, cols], nk)
                    c1a = _row_as_bf16_tile(c1_ref[h, j, ar:ar + 1, :][:, cols], nk)
                    term = jnp.where(rank_ref[h, :, cols] < n1a, e2_ref[h, :, cols], zero) * c1a
                    wsum = term if wsum is None else wsum + term
                g = _gelu_tanh(act[ai * nk:(ai + 1) * nk, cols].astype(bf16))
                dst[ar * nk:(ar + 1) * nk, cols] = wsum * g

        lw = tm // vsplit
        act_next = scores(0)
        for s in range(nsub):
            act = act_next
            if s + 1 < nsub:
                act_next = scores(s + 1)
            for lc in range(vsplit // 2):
                masks(act, s, lc, lw)
            drain(src, s, nsub)
            for lc in range(vsplit // 2, vsplit):
                masks(act, s, lc, lw)

    @pl.when(jnp.logical_and(j < ne, j % 2 == 0))
    def _():
        build(coef_a, coef_b)

    @pl.when(jnp.logical_and(j < ne, j % 2 == 1))
    def _():
        build(coef_b, coef_a)

    @pl.when(j == ne)
    def _():
        drain(coef_b if ne % 2 == 0 else coef_a, 0, 1)
        o_ref[...] = h_ref[...] + acc_scr[...].T


def peer_dense(xnt, u, vt, rank2, n1, c1, e2, h, *, tm, te, sub, vsplit=1):
    n, d = h.shape
    n_exp = u.shape[0]
    ne = n_exp // te
    assert te // PEER_N_KEYS == SUBLANES
    sel_spec = pl.BlockSpec((PEER_HEADS, PEER_N_KEYS, tm), lambda i, j: (0, 0, i))
    n1 = n1.reshape(PEER_HEADS, ne, SUBLANES, n)
    c1 = c1.reshape(PEER_HEADS, ne, SUBLANES, n)
    row_spec = pl.BlockSpec((PEER_HEADS, ne, SUBLANES, tm), lambda i, j: (0, 0, 0, i))
    return pl.pallas_call(
        functools.partial(_peer_dense_kernel, te=te, sub=sub, ne=ne, vsplit=vsplit),
        out_shape=jax.ShapeDtypeStruct((n, d), f32),
        grid=(n // tm, ne + 1),
        in_specs=[pl.BlockSpec((d, tm), lambda i, j: (0, i)),
                  pl.BlockSpec((te, d), lambda i, j: (jnp.minimum(j, ne - 1), 0)),
                  pl.BlockSpec((d, te), lambda i, j: (0, jnp.maximum(j - 1, 0))),
                  sel_spec, row_spec, row_spec, sel_spec,
                  pl.BlockSpec((tm, d), lambda i, j: (i, 0))],
        out_specs=pl.BlockSpec((tm, d), lambda i, j: (i, 0)),
        scratch_shapes=[pltpu.VMEM((d, tm), f32), pltpu.VMEM((te, tm), bf16), pltpu.VMEM((te, tm), bf16)],
        compiler_params=_cparams(("parallel", "arbitrary")),
        name="peer_dense",
    )(xnt, u, vt, rank2, n1, c1, e2, h)


ATTN_BLOCK = 128


def _attn_band_kernel(*refs, dil):
    nq = Q_PER_KV * KV_PER_BRANCH * HEAD_DIM // LANES
    q_refs = refs[:nq]
    kc_ref, kp_ref, vc_ref, vp_ref, o_ref, l_ref, o_scr, l_scr = refs[nq:]
    i = pl.program_id(1)
    blk = ATTN_BLOCK
    scale = HEAD_DIM ** -0.5
    rows = Q_PER_KV * blk
    a_i = lax.broadcasted_iota(jnp.int32, (rows, blk), 0) % blk
    c_i = lax.broadcasted_iota(jnp.int32, (rows, blk), 1)
    mask_c = c_i <= a_i
    mask_p = jnp.logical_and(c_i >= a_i, i > 0)

    def residue(r, carry):
        sel = pl.ds(r, blk, stride=dil) if dil > 1 else pl.ds(0, blk)
        q = jnp.concatenate([qr[sel, :] for qr in q_refs], axis=1) * scale
        kc, kp, vc, vp = kc_ref[sel, :], kp_ref[sel, :], vc_ref[sel, :], vp_ref[sel, :]
        outs, lses = [], []
        for j in range(KV_PER_BRANCH):
            hs = slice(j * HEAD_DIM, (j + 1) * HEAD_DIM)
            qs = jnp.concatenate([q[:, (j * Q_PER_KV + x) * HEAD_DIM:(j * Q_PER_KV + x + 1) * HEAD_DIM]
                                  for x in range(Q_PER_KV)], axis=0).astype(bf16)
            s_c = jnp.where(mask_c, _dot_nt(qs, kc[:, hs].astype(bf16)), NEG_INF)
            s_p = jnp.where(mask_p, _dot_nt(qs, kp[:, hs].astype(bf16)), NEG_INF)
            m = jnp.maximum(jnp.max(s_c, axis=-1, keepdims=True), jnp.max(s_p, axis=-1, keepdims=True))
            p_c = jnp.exp(s_c - m)
            p_p = jnp.exp(s_p - m)
            l = jnp.sum(p_c, axis=-1, keepdims=True) + jnp.sum(p_p, axis=-1, keepdims=True)
            o = jnp.dot(p_c.astype(bf16), vc[:, hs].astype(bf16), preferred_element_type=f32)
            o = o + jnp.dot(p_p.astype(bf16), vp[:, hs].astype(bf16), preferred_element_type=f32)
            o = o / l
            lse = jnp.broadcast_to(m + jnp.log(l), (rows, HEAD_DIM))
            for x in range(Q_PER_KV):
                outs.append(o[x * blk:(x + 1) * blk])
                lses.append(lse[x * blk:(x + 1) * blk])
        per = LANES // HEAD_DIM
        for cb in range(nq):
            o_scr[cb, sel, :] = jnp.concatenate(outs[cb * per:(cb + 1) * per], axis=1)
            l_scr[cb, sel, :] = jnp.concatenate(lses[cb * per:(cb + 1) * per], axis=1)
        return carry

    lax.fori_loop(0, dil, residue, 0)
    for cb in range(nq):
        o_ref[:, cb * LANES:(cb + 1) * LANES] = o_scr[cb]
        l_ref[:, cb * LANES:(cb + 1) * LANES] = l_scr[cb]


def attn_band(q, k, v, g, dil):
    b, t, _ = q.shape
    span = ATTN_BLOCK * dil
    assert t % span == 0
    hq = Q_PER_KV * KV_PER_BRANCH * HEAD_DIM
    hk = KV_PER_BRANCH * HEAD_DIM
    cur = lambda bi, i: (bi, i, g)
    prev = lambda bi, i: (bi, jnp.maximum(i - 1, 0), g)
    osd = jax.ShapeDtypeStruct((b, t, hq), f32)
    nq = hq // LANES
    q_specs = [pl.BlockSpec((None, span, LANES), functools.partial(lambda bi, i, cb: (bi, i, g * nq + cb), cb=cb))
               for cb in range(nq)]
    return pl.pallas_call(
        functools.partial(_attn_band_kernel, dil=dil),
        out_shape=(osd, osd),
        grid=(b, t // span),
        in_specs=q_specs + [pl.BlockSpec((None, span, hk), cur), pl.BlockSpec((None, span, hk), prev),
                            pl.BlockSpec((None, span, hk), cur), pl.BlockSpec((None, span, hk), prev)],
        out_specs=(pl.BlockSpec((None, span, hq), lambda bi, i: (bi, i, 0)),
                   pl.BlockSpec((None, span, hq), lambda bi, i: (bi, i, 0))),
        scratch_shapes=[pltpu.VMEM((nq, span, LANES), f32), pltpu.VMEM((nq, span, LANES), f32)],
        compiler_params=_cparams(("parallel", "arbitrary")),
        name=f"attn_band_d{dil}",
    )(*([q] * nq), k, k, v, v)


def _merge3_kernel(o0, o1, o2, l0, l1, l2, out_ref):
    a0, a1, a2 = l0[...], l1[...], l2[...]
    m = jnp.maximum(jnp.maximum(a0, a1), a2)
    w0, w1, w2 = jnp.exp(a0 - m), jnp.exp(a1 - m), jnp.exp(a2 - m)
    out_ref[...] = ((w0 * o0[...] + w1 * o1[...] + w2 * o2[...]) / (w0 + w1 + w2)).astype(out_ref.dtype)


def merge3(os_, ls_, *, tm):
    n, c = os_[0].shape
    spec = pl.BlockSpec((tm, c), lambda i: (i, 0))
    return pl.pallas_call(
        _merge3_kernel,
        out_shape=jax.ShapeDtypeStruct((n, c), bf16),
        grid=(n // tm,),
        in_specs=[spec] * 6,
        out_specs=spec,
        compiler_params=_cparams(("parallel",)),
        name="attn_merge",
    )(*os_, *ls_)


def _attn_sample_kernel(q_ref, ck_ref, cv_ref, nk_ref, nv_ref, o_ref, *, t_new, win_buf):
    scale = HEAD_DIM ** -0.5
    q = q_ref[...] * scale
    n_cache = ck_ref.shape[0]
    n_new = nk_ref.shape[0]
    rows = Q_PER_KV * t_new
    t_c = lax.broadcasted_iota(jnp.int32, (rows, n_cache), 0) % t_new
    d_c = win_buf + t_c - lax.broadcasted_iota(jnp.int32, (rows, n_cache), 1)
    t_n = lax.broadcasted_iota(jnp.int32, (rows, n_new), 0) % t_new
    d_n = t_n - lax.broadcasted_iota(jnp.int32, (rows, n_new), 1)
    outs = [None] * (KV_PER_BRANCH * Q_PER_KV)
    for j in range(KV_PER_BRANCH):
        parts = []
        for g, (win, dil) in enumerate(DIL_PATTERNS):
            hq = (g * KV_PER_BRANCH + j) * Q_PER_KV
            hk = slice((g * KV_PER_BRANCH + j) * HEAD_DIM, (g * KV_PER_BRANCH + j + 1) * HEAD_DIM)
            qs = jnp.concatenate([q[:, (hq + r) * HEAD_DIM:(hq + r + 1) * HEAD_DIM] for r in range(Q_PER_KV)],
                                 axis=0).astype(bf16)
            ok_c = (d_c >= 0) & (d_c <= win) & ((d_c & (dil - 1)) == 0)
            ok_n = (d_n >= 0) & (d_n <= win) & ((d_n & (dil - 1)) == 0)
            s_c = jnp.where(ok_c, _dot_nt(qs, ck_ref[:, hk].astype(bf16)), NEG_INF)
            s_n = jnp.where(ok_n, _dot_nt(qs, nk_ref[:, hk].astype(bf16)), NEG_INF)
            parts.append((s_c, s_n, hk))
        m = None
        for s_c, s_n, _ in parts:
            mm = jnp.maximum(jnp.max(s_c, axis=-1, keepdims=True), jnp.max(s_n, axis=-1, keepdims=True))
            m = mm if m is None else jnp.maximum(m, mm)
        l = jnp.zeros_like(m)
        o = jnp.zeros((rows, HEAD_DIM), f32)
        for s_c, s_n, hk in parts:
            p_c = jnp.exp(s_c - m)
            p_n = jnp.exp(s_n - m)
            l = l + jnp.sum(p_c, axis=-1, keepdims=True) + jnp.sum(p_n, axis=-1, keepdims=True)
            o = o + jnp.dot(p_c.astype(bf16), cv_ref[:, hk].astype(bf16), preferred_element_type=f32)
            o = o + jnp.dot(p_n.astype(bf16), nv_ref[:, hk].astype(bf16), preferred_element_type=f32)
        o = o / l
        for r in range(Q_PER_KV):
            outs[j * Q_PER_KV + r] = o[r * t_new:(r + 1) * t_new]
    o_ref[...] = jnp.concatenate(outs, axis=1).astype(o_ref.dtype)


def attn_sample(q, ck, cv, nk, nv):
    b, t_new, qc = q.shape
    w = ck.shape[1]
    kc = ck.shape[2]
    n_new = nk.shape[1]
    return pl.pallas_call(
        functools.partial(_attn_sample_kernel, t_new=t_new, win_buf=w),
        out_shape=jax.ShapeDtypeStruct((b, t_new, ATTN_OUT_DIM), bf16),
        grid=(b,),
        in_specs=[pl.BlockSpec((None, t_new, qc), lambda i: (i, 0, 0)),
                  pl.BlockSpec((None, w, kc), lambda i: (i, 0, 0)),
                  pl.BlockSpec((None, w, kc), lambda i: (i, 0, 0)),
                  pl.BlockSpec((None, n_new, kc), lambda i: (i, 0, 0)),
                  pl.BlockSpec((None, n_new, kc), lambda i: (i, 0, 0))],
        out_specs=pl.BlockSpec((None, t_new, ATTN_OUT_DIM), lambda i: (i, 0, 0)),
        compiler_params=_cparams(("parallel",)),
        name="attn_sample",
    )(q, ck, cv, nk, nv)


def _prep_weights(norm_mix, norm_ffn, norm_ple, ssm_w_in, ssm_conv_w, ssm_conv_b, ssm_dt_bias, ssm_a_log, ssm_d,
                  ssm_norm, ssm_w_out, kv_norm, w_kv, k_norm, w_q, q_norm, w_o,
                  peer_w_query, peer_sub_keys, peer_u, peer_v, ple_w_proj, ple_w_gate):
    zx = D_INNER + CONV_DIM
    w_in = ssm_w_in[0]
    kdim = N_KV_HEADS * HEAD_DIM
    rep = LANES // HEAD_DIM
    return dict(
        norm_mix=norm_mix, norm_ffn=norm_ffn, norm_ple=norm_ple,
        w_z=w_in[:, :D_INNER].astype(bf16), w_xbc=w_in[:, D_INNER:zx].astype(bf16),
        w_dt=jnp.pad(w_in[:, zx:], ((0, 0), (0, LANES - SSM_HEADS))).astype(bf16),
        conv_w=ssm_conv_w[0], conv_b=ssm_conv_b[0], dt_bias=ssm_dt_bias[0], a_log=ssm_a_log[0],
        d_skip=ssm_d[0], ssm_norm=ssm_norm[0], w_out=ssm_w_out[0].astype(bf16),
        kv_norm=kv_norm, w_k=w_kv[:, :kdim].astype(bf16), w_v=w_kv[:, kdim:].astype(bf16),
        k_gain=jnp.tile(k_norm.astype(f32), rep).reshape(1, LANES),
        w_q=w_q[0].astype(bf16), q_gain=jnp.tile(q_norm[0].astype(f32), rep).reshape(1, LANES),
        w_o=w_o[0].astype(bf16),
        peer_wq=[peer_w_query[i].astype(bf16) for i in range(DEPTH)],
        peer_sk=[peer_sub_keys[i].astype(bf16) for i in range(DEPTH)],
        peer_u=[peer_u[i].astype(bf16) for i in range(DEPTH)],
        peer_vt=[peer_v[i].astype(bf16).T for i in range(DEPTH)],
        ple_wp=[ple_w_proj[i].astype(bf16) for i in range(DEPTH)],
        ple_wg=[ple_w_gate[i].astype(bf16) for i in range(DEPTH)],
    )


PEER_TOKEN_TILE = 512
PEER_SCORE_TILE = 256
PEER_EXPERT_TILE = SUBLANES * PEER_N_KEYS
PEER_SUB_TILE = PEER_N_KEYS


def _peer_ple(h, p, i, wts, tm):
    n = h.shape[0]
    tm_d = PEER_TOKEN_TILE if n % PEER_TOKEN_TILE == 0 else n
    xnt, rank2, n1, c1, e2 = peer_scores(h, wts["norm_ffn"][i], wts["peer_wq"][i], wts["peer_sk"][i],
                                         tm=PEER_SCORE_TILE)
    h = peer_dense(xnt, wts["peer_u"][i], wts["peer_vt"][i], rank2, n1, c1, e2, h,
                   tm=tm_d, te=PEER_EXPERT_TILE, sub=PEER_SUB_TILE)
    return fused_matmul(h, wts["ple_wg"][i], gain=wts["norm_ple"][i], mode="ple", res=h,
                        ple=(p, wts["ple_wp"][i]), tm=tm)


def _trunk(x, p, conv_state, ssm_state, past_k, past_v, pos0, wts):
    b, t, d = x.shape
    n = b * t
    prompt = past_k is None
    tm = _token_tile(n)
    h = x.reshape(n, d)

    z = fused_matmul(h, wts["w_z"], gain=wts["norm_mix"][0], tm=tm).reshape(b, t, D_INNER)
    xbc = fused_matmul(h, wts["w_xbc"], gain=wts["norm_mix"][0], tm=tm).reshape(b, t, CONV_DIM)
    dtr = fused_matmul(h, wts["w_dt"], gain=wts["norm_mix"][0], tm=tm)
    new_conv = xbc[:, t - (CONV_W - 1):, :] if t >= CONV_W - 1 else None
    dtr = dtr.reshape(b, t, LANES)
    tp = -(-t // SSD_CHUNK) * SSD_CHUNK
    if tp != t:
        padt = ((0, 0), (0, tp - t), (0, 0))
        z_p, xbc_p, dtr_p = jnp.pad(z, padt), jnp.pad(xbc, padt), jnp.pad(dtr, padt)
    else:
        z_p, xbc_p, dtr_p = z, xbc, dtr
    cb0 = jnp.pad(conv_state.astype(f32), ((0, 0), (SUBLANES - (CONV_W - 1), 0), (0, 0)))
    h0 = ssm_state.astype(f32).reshape(b, SSM_HEADS * SSM_HEAD_DIM, D_STATE)
    y, h_fin = ssd_mixer(z_p, xbc_p, dtr_p, h0, cb0, wts["conv_w"], wts["conv_b"], wts["dt_bias"], wts["a_log"],
                         wts["d_skip"], wts["ssm_norm"], t_valid=(None if tp == t else t))
    y = y[:, :t].reshape(n, D_INNER)
    h = fused_matmul(y, wts["w_out"], mode="add", res=h, tm=tm)
    new_ssm = h_fin.reshape(b, SSM_HEADS, SSM_HEAD_DIM, D_STATE)
    h = _peer_ple(h, p[0].reshape(n, -1), 0, wts, tm)

    pos = pos0 + jnp.arange(t, dtype=jnp.int32)
    tabs = _rope_tables(pos)
    if t < tm:
        tabs = tuple(jnp.tile(a, (tm // t, 1)) for a in tabs)
    k_new = fused_matmul(h, wts["w_k"], gain=wts["kv_norm"], mode="headrope",
                         headrope=(wts["k_gain"],) + tabs, tm=tm)
    v_new = fused_matmul(h, wts["w_v"], gain=wts["kv_norm"], tm=tm)

    q = fused_matmul(h, wts["w_q"], gain=wts["norm_mix"][1], mode="headrope",
                     headrope=(wts["q_gain"],) + tabs, tm=tm)
    kdim = N_KV_HEADS * HEAD_DIM
    if prompt:
        q3, k3, v3 = q.reshape(b, t, -1), k_new.reshape(b, t, kdim), v_new.reshape(b, t, kdim)
        os_, ls_ = [], []
        for g, (_, dil) in enumerate(DIL_PATTERNS):
            o_g, l_g = attn_band(q3, k3, v3, g, dil)
            os_.append(o_g.reshape(n, ATTN_OUT_DIM))
            ls_.append(l_g.reshape(n, ATTN_OUT_DIM))
        att = merge3(os_, ls_, tm=tm)
    else:
        padn = ((0, 0), (0, SSD_CHUNK - t), (0, 0))
        nk = jnp.pad(k_new.reshape(b, t, kdim), padn)
        nv = jnp.pad(v_new.reshape(b, t, kdim), padn)
        w = past_k.shape[1]
        att = attn_sample(q.reshape(b, t, -1), past_k.reshape(b, w, kdim).astype(f32),
                          past_v.reshape(b, w, kdim).astype(f32), nk, nv).reshape(n, ATTN_OUT_DIM)
    h = fused_matmul(att, wts["w_o"], mode="add", res=h, tm=tm)
    h = _peer_ple(h, p[1].reshape(n, -1), 1, wts, tm)

    return (h.reshape(b, t, d), new_conv[None], new_ssm[None],
            k_new.reshape(b, t, N_KV_HEADS, HEAD_DIM), v_new.reshape(b, t, N_KV_HEADS, HEAD_DIM))


def kernel(x_prompt, x_sample, state_conv, state_ssm, cache_k, cache_v, p_prompt, p_sample, norm_mix, norm_ffn, norm_ple, ssm_w_in, ssm_conv_w, ssm_conv_b, ssm_dt_bias, ssm_a_log, ssm_d, ssm_norm, ssm_w_out, kv_norm, w_kv, k_norm, w_q, q_norm, w_o, peer_w_query, peer_sub_keys, peer_u, peer_v, ple_w_proj, ple_w_gate):
    wts = _prep_weights(norm_mix, norm_ffn, norm_ple, ssm_w_in, ssm_conv_w, ssm_conv_b, ssm_dt_bias, ssm_a_log,
                        ssm_d, ssm_norm, ssm_w_out, kv_norm, w_kv, k_norm, w_q, q_norm, w_o,
                        peer_w_query, peer_sub_keys, peer_u, peer_v, ple_w_proj, ple_w_gate)
    b_p, t_p, _ = x_prompt.shape
    zero_conv = jnp.zeros((b_p, CONV_W - 1, CONV_DIM), x_prompt.dtype)
    zero_ssm = jnp.zeros((b_p, SSM_HEADS, SSM_HEAD_DIM, D_STATE), x_prompt.dtype)
    y_p, conv_p, ssm_p, k_p, v_p = _trunk(x_prompt, p_prompt, zero_conv, zero_ssm, None, None, 0, wts)
    keep = min(max(w for w, _ in DIL_PATTERNS), t_p)
    past_len = 16384
    y_s, conv_s, ssm_s, k_s, v_s = _trunk(x_sample, p_sample, state_conv[0], state_ssm[0], cache_k, cache_v,
                                          past_len, wts)
    return (y_p, y_s, conv_p, ssm_p, k_p[:, -keep:], v_p[:, -keep:], conv_s, ssm_s, k_s, v_s)
```

```python
import functools
import math

import jax
import jax.numpy as jnp
from jax import lax
from jax.experimental import pallas as pl
from jax.experimental.pallas import tpu as pltpu

f32 = jnp.float32
bf16 = jnp.bfloat16

D_MODEL = 1024
DEPTH = 2
N_A_LAYERS = 1
D_INNER = 2048
SSM_HEAD_DIM = 64
SSM_HEADS = 32
SSM_GROUPS = 8
SSM_HEADS_PER_GROUP = 4
D_STATE = 128
CONV_W = 4
CONV_DIM = 4096
SSD_CHUNK = 128
HEAD_DIM = 64
DIL_PATTERNS = ((128, 1), (512, 4), (2048, 16))
N_DIL = 3
KV_PER_BRANCH = 2
Q_PER_KV = 4
N_Q_HEADS = 24
N_KV_HEADS = 6
ATTN_OUT_DIM = 512
ROPE_DIM = 16
ROPE_THETA = 500000.0
PEER_HEADS = 8
PEER_N_KEYS = 128
PEER_TOPK = 16
PEER_HALF = 128
EPS = 1e-6

LANES = 128
SUBLANES = 8
VMEM_LIMIT_BYTES = 56 * 1024 * 1024

NEG_INF = float("-inf")


def _cparams(sem):
    return pltpu.CompilerParams(dimension_semantics=sem, vmem_limit_bytes=VMEM_LIMIT_BYTES)


def _dot_nt(a, b):
    return lax.dot_general(a, b, (((1,), (1,)), ((), ())), preferred_element_type=f32)


def _dot_tn(a, b):
    return lax.dot_general(a, b, (((0,), (0,)), ((), ())), preferred_element_type=f32)


def _sigmoid(x):
    return 1.0 / (1.0 + jnp.exp(-x))


def _silu(x):
    return x * _sigmoid(x)


def _softplus(x):
    return jnp.maximum(x, 0.0) + jnp.log1p(jnp.exp(-jnp.abs(x)))


def _gelu_tanh(x):
    c = math.sqrt(2.0 / math.pi)
    return x * (0.5 + 0.5 * jnp.tanh(x * (c + (c * 0.044715) * (x * x))))


def _fmm_kernel(*refs, has_norm, mode):
    it = iter(refs)
    x_ref = next(it)
    g_ref = next(it) if has_norm else None
    w_ref = next(it)
    res_ref = next(it) if mode in ("add", "ple") else None
    if mode == "ple":
        p_ref, wp_ref = next(it), next(it)
    if mode == "headrope":
        hg_ref, ta_ref, tb_ref, tc_ref, bd_ref = next(it), next(it), next(it), next(it), next(it)
    o_ref = next(it)
    xn_ref = next(it)

    @pl.when(pl.program_id(1) == 0)
    def _():
        x = x_ref[...].astype(f32)
        if has_norm:
            ms = jnp.mean(x * x, axis=-1, keepdims=True)
            x = x * lax.rsqrt(ms + EPS) * g_ref[...]
        xn_ref[...] = x.astype(bf16)

    acc = jnp.dot(xn_ref[...], w_ref[...], preferred_element_type=f32)
    if mode == "plain":
        o_ref[...] = acc.astype(o_ref.dtype)
    elif mode == "add":
        o_ref[...] = res_ref[...] + acc
    elif mode == "ple":
        pp = jnp.dot(p_ref[...].astype(bf16), wp_ref[...], preferred_element_type=f32)
        o_ref[...] = res_ref[...] + pp * _sigmoid(acc)
    elif mode == "headrope":
        tn = acc.shape[1]
        ta, tb, tc = ta_ref[...], tb_ref[...], tc_ref[...]
        hg = hg_ref[...]
        bd = bd_ref[...]
        for nb in range(tn // LANES):
            blk = acc[:, nb * LANES:(nb + 1) * LANES]
            ms = jnp.dot(blk * blk, bd, precision=lax.Precision.HIGHEST, preferred_element_type=f32)
            y = blk * lax.rsqrt(ms + EPS) * hg
            y = y * ta + pltpu.roll(y, LANES - ROPE_DIM // 2, 1) * tb + pltpu.roll(y, ROPE_DIM // 2, 1) * tc
            o_ref[:, nb * LANES:(nb + 1) * LANES] = y


MATMUL_VMEM_BUDGET = 40 * 1024 * 1024


def _token_tile(m):
    return next((c for c in (1024, 512, 256, 128) if m % c == 0), m)


def _col_tile(tm, k, n, x_bytes, mode, kp):
    out_blocks = 2 if mode in ("add", "ple") else 1
    for tn in sorted({n, 2048, 1536, 1024, 768, 512, 384, 256, 128}, reverse=True):
        if tn > n or n % tn:
            continue
        use = 2 * tm * k * x_bytes + tm * k * 2 + 2 * k * tn * 2 + 2 * out_blocks * tm * tn * 4
        if mode == "ple":
            use += 2 * tm * kp * 4 + 2 * kp * tn * 2
        if use <= MATMUL_VMEM_BUDGET:
            return tn
    raise ValueError("no column tile fits")


def fused_matmul(x, w, *, gain=None, mode="plain", res=None, ple=None, headrope=None, tm, tn=None, out_dtype=f32):
    m, k = x.shape
    n = w.shape[1]
    if tn is None:
        tn = _col_tile(tm, k, n, x.dtype.itemsize, mode, ple[0].shape[1] if ple else 0)
    assert m % tm == 0 and n % tn == 0, (m, tm, n, tn)
    has_norm = gain is not None
    args = [x]
    in_specs = [pl.BlockSpec((tm, k), lambda i, j: (i, 0))]
    if has_norm:
        args.append(gain.reshape(1, k).astype(f32))
        in_specs.append(pl.BlockSpec((1, k), lambda i, j: (0, 0)))
    args.append(w)
    in_specs.append(pl.BlockSpec((k, tn), lambda i, j: (0, j)))
    if mode in ("add", "ple"):
        args.append(res)
        in_specs.append(pl.BlockSpec((tm, tn), lambda i, j: (i, j)))
    if mode == "ple":
        p, wp = ple
        kp = p.shape[1]
        args += [p, wp]
        in_specs += [pl.BlockSpec((tm, kp), lambda i, j: (i, 0)), pl.BlockSpec((kp, tn), lambda i, j: (0, j))]
    if mode == "headrope":
        hg, ta, tb, tc = headrope
        period = ta.shape[0] // tm
        args += [hg, ta, tb, tc, _head_mean_matrix()]
        in_specs.append(pl.BlockSpec((1, LANES), lambda i, j: (0, 0)))
        for _ in range(3):
            in_specs.append(pl.BlockSpec((tm, LANES), lambda i, j: (i % period, 0)))
        in_specs.append(pl.BlockSpec((LANES, LANES), lambda i, j: (0, 0)))
    return pl.pallas_call(
        functools.partial(_fmm_kernel, has_norm=has_norm, mode=mode),
        out_shape=jax.ShapeDtypeStruct((m, n), out_dtype),
        grid=(m // tm, n // tn),
        in_specs=in_specs,
        out_specs=pl.BlockSpec((tm, tn), lambda i, j: (i, j)),
        scratch_shapes=[pltpu.VMEM((tm, k), bf16)],
        compiler_params=_cparams(("parallel", "arbitrary")),
        name="fused_matmul_" + mode,
    )(*args)


def _head_mean_matrix():
    r = jnp.arange(LANES) // HEAD_DIM
    return (r[:, None] == r[None, :]).astype(f32) / HEAD_DIM


def _rope_tables(pos):
    half = ROPE_DIM // 2
    inv = ROPE_THETA ** (-jnp.arange(0, ROPE_DIM, 2, dtype=f32) / ROPE_DIM)
    ang = pos.astype(f32)[:, None] * inv[None, :]
    cos, sin = jnp.cos(ang), jnp.sin(ang)
    t = pos.shape[0]
    ones = jnp.ones((t, HEAD_DIM - ROPE_DIM), f32)
    zeros_h = jnp.zeros((t, half), f32)
    zeros_r = jnp.zeros((t, HEAD_DIM - ROPE_DIM), f32)
    ta = jnp.concatenate([cos, cos, ones], axis=1)
    tb = jnp.concatenate([-sin, zeros_h, zeros_r], axis=1)
    tc = jnp.concatenate([zeros_h, sin, zeros_r], axis=1)
    rep = LANES // HEAD_DIM
    return tuple(jnp.tile(a, (1, rep)) for a in (ta, tb, tc))


def _ssd_kernel(z_ref, xbc_ref, dt_ref, h0_ref, cb0_ref, cw_ref, cbias_ref, dtb_ref, alog_ref, dsk_ref, ng_ref,
                ex_ref, y_ref, hout_ref, h_scr, tail_scr, xc_scr, *, chunk, t_valid):
    c = pl.program_id(1)
    nc = pl.num_programs(1)
    L = chunk
    hp = SSM_HEADS_PER_GROUP * SSM_HEAD_DIM

    @pl.when(c == 0)
    def _():
        for g in range(SSM_GROUPS):
            h_scr[g] = h0_ref[g * hp:(g + 1) * hp, :].T
        tail_scr[0:SUBLANES, :] = cb0_ref[...]

    w = cw_ref[...]
    tail_scr[SUBLANES:2 * SUBLANES, :] = xbc_ref[0:SUBLANES, :]
    for lo, hi, src, off in ((0, SUBLANES, tail_scr, SUBLANES), (SUBLANES, L, xbc_ref, 0)):
        acc = src[lo + off:hi + off, :] * w[CONV_W - 1:CONV_W, :] + cbias_ref[...]
        for s in range(1, CONV_W):
            acc = acc + src[lo + off - s:hi + off - s, :] * w[CONV_W - 1 - s:CONV_W - s, :]
        xc_scr[lo:hi, :] = _silu(acc)
    tail_scr[0:SUBLANES, :] = xbc_ref[L - SUBLANES:L, :]

    dt = _softplus(dt_ref[...] + dtb_ref[...])
    if t_valid is not None:
        rowid = c * L + lax.broadcasted_iota(jnp.int32, (L, LANES), 0)
        dt = jnp.where(rowid < t_valid, dt, 0.0)
    a = -jnp.exp(alog_ref[...])
    da = dt * a
    r_i = lax.broadcasted_iota(jnp.int32, (L, L), 0)
    c_i = lax.broadcasted_iota(jnp.int32, (L, L), 1)
    causal = r_i >= c_i
    cum = jnp.dot(causal.astype(f32), da, precision=lax.Precision.HIGHEST, preferred_element_type=f32)
    cum_t = cum.T
    cum_last = cum[L - 1:L, :]
    tailw = jnp.exp(cum_last - cum) * dt
    ecum = jnp.exp(cum)

    src = jnp.concatenate([dt, tailw, ecum], axis=0)
    src_hi = src.astype(bf16)
    src_lo = (src - src_hi.astype(f32)).astype(bf16)
    pieces = jnp.concatenate([src_hi, src_lo], axis=0)

    for g in range(SSM_GROUPS):
        gs = slice(g * hp, (g + 1) * hp)
        ex = jnp.dot(pieces, ex_ref[:, gs], preferred_element_type=f32)
        dt_x = ex[0:L] + ex[3 * L:4 * L]
        tw_x = ex[L:2 * L] + ex[4 * L:5 * L]
        ec_x = ex[2 * L:3 * L] + ex[5 * L:6 * L]
        x_g = xc_scr[:, gs]
        b_g = xc_scr[:, D_INNER + g * D_STATE:D_INNER + (g + 1) * D_STATE].astype(bf16)
        c_g = xc_scr[:, D_INNER + SSM_GROUPS * D_STATE + g * D_STATE:
                     D_INNER + SSM_GROUPS * D_STATE + (g + 1) * D_STATE].astype(bf16)
        cb = _dot_nt(c_g, b_g)
        ht_g = h_scr[g]
        y_in = jnp.dot(c_g, ht_g.astype(bf16), preferred_element_type=f32)
        xdt = (x_g * dt_x).astype(bf16)
        ys = []
        for hh in range(SSM_HEADS_PER_GROUP):
            h = g * SSM_HEADS_PER_GROUP + hh
            seg = cum[:, h:h + 1] - cum_t[h:h + 1, :]
            dec = jnp.exp(jnp.where(causal, seg, NEG_INF))
            m = (cb * dec).astype(bf16)
            ys.append(jnp.dot(m, xdt[:, hh * SSM_HEAD_DIM:(hh + 1) * SSM_HEAD_DIM], preferred_element_type=f32))
        yg = jnp.concatenate(ys, axis=1) + y_in * ec_x + x_g * dsk_ref[:, gs]
        upd = _dot_tn(b_g, (x_g * tw_x).astype(bf16))
        h_scr[g] = ht_g * ec_x[L - 1:L, :] + upd
        yg = yg * _silu(z_ref[:, gs])
        yg = yg * lax.rsqrt(jnp.mean(yg * yg, axis=-1, keepdims=True) + EPS)
        y_ref[:, gs] = (yg * ng_ref[:, gs]).astype(y_ref.dtype)

    @pl.when(c == nc - 1)
    def _():
        for g in range(SSM_GROUPS):
            hout_ref[g * hp:(g + 1) * hp, :] = h_scr[g].T


def ssd_mixer(z, xbc, dtr, h0, cb0, conv_w, conv_b, dt_bias, a_log, d_skip, norm_g, *, t_valid):
    b, t, _ = z.shape
    L = SSD_CHUNK
    assert t % L == 0
    nc = t // L
    hp_all = SSM_HEADS * SSM_HEAD_DIM
    pad_h = LANES - SSM_HEADS

    def lane_pad(v):
        return jnp.pad(v.astype(f32), (0, pad_h)).reshape(1, LANES)

    cw = jnp.pad(conv_w.astype(f32), ((0, SUBLANES - CONV_W), (0, 0)))
    head_of_col = jnp.arange(hp_all) // SSM_HEAD_DIM
    expand = (jnp.arange(LANES)[:, None] == head_of_col[None, :]).astype(bf16)
    dsk_cols = jnp.repeat(d_skip.astype(f32), SSM_HEAD_DIM).reshape(1, hp_all)
    const = lambda shape: pl.BlockSpec(shape, lambda i, j: (0,) * len(shape))
    return pl.pallas_call(
        functools.partial(_ssd_kernel, chunk=L, t_valid=t_valid),
        out_shape=(jax.ShapeDtypeStruct((b, t, D_INNER), bf16),
                   jax.ShapeDtypeStruct((b, hp_all, D_STATE), f32)),
        grid=(b, nc),
        in_specs=[
            pl.BlockSpec((None, L, D_INNER), lambda i, j: (i, j, 0)),
            pl.BlockSpec((None, L, CONV_DIM), lambda i, j: (i, j, 0)),
            pl.BlockSpec((None, L, LANES), lambda i, j: (i, j, 0)),
            pl.BlockSpec((None, hp_all, D_STATE), lambda i, j: (i, 0, 0)),
            pl.BlockSpec((None, SUBLANES, CONV_DIM), lambda i, j: (i, 0, 0)),
            const((SUBLANES, CONV_DIM)),
            const((1, CONV_DIM)),
            const((1, LANES)), const((1, LANES)), const((1, D_INNER)),
            const((1, D_INNER)),
            const((LANES, hp_all)),
        ],
        out_specs=(pl.BlockSpec((None, L, D_INNER), lambda i, j: (i, j, 0)),
                   pl.BlockSpec((None, hp_all, D_STATE), lambda i, j: (i, 0, 0))),
        scratch_shapes=[pltpu.VMEM((SSM_GROUPS, D_STATE, hp_all // SSM_GROUPS), f32),
                        pltpu.VMEM((2 * SUBLANES, CONV_DIM), f32),
                        pltpu.VMEM((L, CONV_DIM), f32)],
        compiler_params=_cparams(("parallel", "arbitrary")),
        name="ssd_mixer",
    )(z, xbc, dtr, h0, cb0, cw, conv_b.reshape(1, CONV_DIM).astype(f32),
      lane_pad(dt_bias), lane_pad(a_log), dsk_cols, norm_g.reshape(1, D_INNER).astype(f32), expand)


def _sort_network(n):
    def merge(lo, hi, r):
        step = r * 2
        if step < hi - lo:
            yield from merge(lo, hi, step)
            yield from merge(lo + r, hi, step)
            yield from [(i, i + r) for i in range(lo + r, hi - r, step)]
        else:
            yield (lo, lo + r)

    def sort(lo, hi):
        if hi - lo >= 1:
            mid = lo + (hi - lo) // 2
            yield from sort(lo, mid)
            yield from sort(mid + 1, hi)
            yield from merge(lo, hi, 1)

    return list(sort(0, n - 1))


def _sublane_max(x):
    for shift in (4, 2, 1):
        x = jnp.maximum(x, pltpu.roll(x, shift, 0))
    return x


def _pop_columns(cols, extra, n):
    cols = list(cols)
    out = []
    for r in range(n):
        top = cols[0] if extra is None else jnp.maximum(cols[0], extra)
        m = _sublane_max(top)
        out.append(m)
        if r + 1 == n:
            break
        hit = cols[0] == m
        for i in range(min(n - r - 1, len(cols))):
            below = cols[i + 1] if i + 1 < len(cols) else NEG_INF
            cols[i] = jnp.where(hit, below, cols[i])
        if extra is not None:
            extra = jnp.where(extra == m, NEG_INF, extra)
    return out


def _top_values(s, n):
    k = s.shape[0] // SUBLANES
    v = [s[i * SUBLANES:(i + 1) * SUBLANES] for i in range(k)]
    for i, j in _sort_network(k):
        v[i], v[j] = jnp.maximum(v[i], v[j]), jnp.minimum(v[i], v[j])
    return _pop_columns(v[:n], None, n)


def _bf16_pair_words(x):
    hi = pltpu.bitcast(x.astype(bf16).astype(f32), jnp.uint32)
    return hi | (hi >> 16)


def _row_as_bf16_tile(words, rows):
    packed = pltpu.bitcast(jnp.broadcast_to(words, (SUBLANES, words.shape[1])), bf16)
    return jnp.concatenate([packed] * (rows // packed.shape[0]), axis=0)


def _peer_scores_kernel(h_ref, g_ref, wq_ref, sk_ref, xnt_ref, rank_ref, n1_ref, c1_ref, e2_ref, q_scr):
    x = h_ref[...]
    ms = jnp.mean(x * x, axis=-1, keepdims=True)
    xn = x * lax.rsqrt(ms + EPS) * g_ref[...]
    xnt_ref[...] = xn.T.astype(bf16)
    q = jnp.dot(xn.astype(bf16), wq_ref[...], preferred_element_type=f32).astype(bf16)
    for i in range(2 * PEER_HEADS):
        q_scr[i] = q[:, i * PEER_HALF:(i + 1) * PEER_HALF]
    kk = PEER_TOPK
    sk0 = sk_ref[0]
    sk1 = sk_ref[1]

    nkeys = sk0.shape[0]
    tokens = h_ref.shape[0]

    def rep(tile):
        return jnp.concatenate([tile] * (nkeys // SUBLANES), axis=0)

    def select_half(s1, s2):
        av = _top_values(s1, kk)
        bv = _top_values(s2, kk)
        sub = lax.broadcasted_iota(jnp.int32, av[0].shape, 0)
        a_lo, a_hi = av[SUBLANES - 1], av[2 * SUBLANES - 1]
        for r in range(SUBLANES - 2, -1, -1):
            a_lo = jnp.where(sub == r, av[r], a_lo)
            a_hi = jnp.where(sub == r, av[SUBLANES + r], a_hi)
        cand = [a_lo + bv[0]]
        for c in range(1, kk):
            cand.append(jnp.where(sub < kk // (c + 1), a_lo + bv[c], NEG_INF))
        best = _pop_columns(cand, a_hi + bv[0], kk)
        tau, top = best[kk - 1], best[0]
        zsum = jnp.zeros_like(top)
        for r in range(kk):
            zsum = zsum + jnp.exp(best[r] - top)
        tau_r = rep(tau)
        rank2 = jnp.full(s2.shape, float(kk), f32)
        n1 = jnp.zeros(s1.shape, f32)
        for c in range(kk - 1, -1, -1):
            b_c = rep(bv[c])
            rank2 = jnp.where(s2 >= b_c, float(c), rank2)
            n1 = n1 + jnp.where(s1 + b_c >= tau_r, 1.0, 0.0)
        c1 = jnp.exp(s1 - rep(av[0])) / rep(zsum)
        e2 = jnp.exp(s2 - rep(bv[0]))
        return rank2, n1, c1, e2

    def head(h, carry):
        s1 = _dot_nt(sk0, q_scr[2 * h])
        s2 = _dot_nt(sk1, q_scr[2 * h + 1])
        parts = [select_half(s1[:, l0:l0 + LANES], s2[:, l0:l0 + LANES]) for l0 in range(0, tokens, LANES)]
        rank2, n1, c1, e2 = (jnp.concatenate(p, axis=1) for p in zip(*parts))
        rank_ref[h] = rank2.astype(rank_ref.dtype)
        n1_ref[h] = _bf16_pair_words(n1)
        c1_ref[h] = _bf16_pair_words(c1)
        e2_ref[h] = e2.astype(e2_ref.dtype)
        return carry

    lax.fori_loop(0, PEER_HEADS, head, 0)


def peer_scores(h, gain, wq, sk, *, tm):
    n, d = h.shape
    nq = wq.shape[1]
    sel16 = jax.ShapeDtypeStruct((PEER_HEADS, PEER_N_KEYS, n), bf16)
    sel32 = jax.ShapeDtypeStruct((PEER_HEADS, PEER_N_KEYS, n), jnp.uint32)
    sel_spec = pl.BlockSpec((PEER_HEADS, PEER_N_KEYS, tm), lambda i: (0, 0, i))
    return pl.pallas_call(
        _peer_scores_kernel,
        out_shape=(jax.ShapeDtypeStruct((d, n), bf16), sel16, sel32, sel32, sel16),
        grid=(n // tm,),
        in_specs=[pl.BlockSpec((tm, d), lambda i: (i, 0)),
                  pl.BlockSpec((1, d), lambda i: (0, 0)),
                  pl.BlockSpec((d, nq), lambda i: (0, 0)),
                  pl.BlockSpec((2, PEER_N_KEYS, PEER_HALF), lambda i: (0, 0, 0))],
        out_specs=(pl.BlockSpec((d, tm), lambda i: (0, i)), sel_spec, sel_spec, sel_spec, sel_spec),
        scratch_shapes=[pltpu.VMEM((2 * PEER_HEADS, tm, PEER_HALF), bf16)],
        compiler_params=_cparams(("parallel",)),
        name="peer_scores",
    )(h, gain.reshape(1, d).astype(f32), wq, sk)


def _peer_dense_kernel(xnt_ref, u_ref, vt_ref, rank_ref, n1_ref, c1_ref, e2_ref, h_ref, o_ref,
                       acc_scr, coef_a, coef_b, *, te, sub, ne, vsplit):
    j = pl.program_id(1)
    nk = PEER_N_KEYS
    a_per_step = te // nk
    a_per_sub = sub // nk
    tm = xnt_ref.shape[1]
    pk = 2 * SUBLANES
    zero = jnp.zeros((), bf16)

    @pl.when(j == 0)
    def _():
        acc_scr[...] = jnp.zeros_like(acc_scr)
        coef_b[...] = jnp.zeros_like(coef_b)

    def drain(src, part, parts):
        d = acc_scr.shape[0]
        rows = slice(part * d // parts, (part + 1) * d // parts)
        acc_scr[rows, :] += jnp.dot(vt_ref[rows, :], src[...], preferred_element_type=f32)

    def build(dst, src):
        xnt = xnt_ref[...]
        nsub = te // sub

        def scores(s):
            return jnp.dot(u_ref[s * sub:(s + 1) * sub, :], xnt, preferred_element_type=f32)

        def masks(act, s, lc, lw):
            cols = slice(lc * lw, (lc + 1) * lw)
            for ai in range(a_per_sub):
                ar = s * a_per_sub + ai
                grp = j * (a_per_step // SUBLANES) + ar // SUBLANES
                row = ar % SUBLANES
                wsum = None
                for h in range(PEER_HEADS):
                    n1a = _row_as_bf16_tile(n1_ref[h, grp, row:row + 1, :][:, cols], nk)
                    c1a = _row_as_bf16_tile(c1_ref[h, grp, row:row + 1, :][:, cols], nk)
                    term = jnp.where(rank_ref[h, :, cols] < n1a, e2_ref[h, :, cols], zero) * c1a
                    wsum = term if wsum is None else wsum + term
                g = _gelu_tanh(act[ai * nk:(ai + 1) * nk, cols].astype(bf16))
                dst[ar * nk:(ar + 1) * nk, cols] = wsum * g

        lw = tm // vsplit
        act_next = scores(0)
        for s in range(nsub):
            act = act_next
            if s + 1 < nsub:
                act_next = scores(s + 1)
            for lc in range(vsplit // 2):
                masks(act, s, lc, lw)
            drain(src, s, nsub)
            for lc in range(vsplit // 2, vsplit):
                masks(act, s, lc, lw)

    @pl.when(jnp.logical_and(j < ne, j % 2 == 0))
    def _():
        build(coef_a, coef_b)

    @pl.when(jnp.logical_and(j < ne, j % 2 == 1))
    def _():
        build(coef_b, coef_a)

    @pl.when(j == ne)
    def _():
        drain(coef_b if ne % 2 == 0 else coef_a, 0, 1)
        o_ref[...] = h_ref[...] + acc_scr[...].T


def peer_dense(xnt, u, vt, rank2, n1, c1, e2, h, *, tm, te, sub, vsplit=1):
    n, d = h.shape
    n_exp = u.shape[0]
    ne = n_exp // te
    assert (te // PEER_N_KEYS) % SUBLANES == 0
    sel_spec = pl.BlockSpec((PEER_HEADS, PEER_N_KEYS, tm), lambda i, j: (0, 0, i))
    ngrp = PEER_N_KEYS // SUBLANES
    n1 = n1.reshape(PEER_HEADS, ngrp, SUBLANES, n)
    c1 = c1.reshape(PEER_HEADS, ngrp, SUBLANES, n)
    row_spec = pl.BlockSpec((PEER_HEADS, ngrp, SUBLANES, tm), lambda i, j: (0, 0, 0, i))
    return pl.pallas_call(
        functools.partial(_peer_dense_kernel, te=te, sub=sub, ne=ne, vsplit=vsplit),
        out_shape=jax.ShapeDtypeStruct((n, d), f32),
        grid=(n // tm, ne + 1),
        in_specs=[pl.BlockSpec((d, tm), lambda i, j: (0, i)),
                  pl.BlockSpec((te, d), lambda i, j: (jnp.minimum(j, ne - 1), 0)),
                  pl.BlockSpec((d, te), lambda i, j: (0, jnp.maximum(j - 1, 0))),
                  sel_spec, row_spec, row_spec, sel_spec,
                  pl.BlockSpec((tm, d), lambda i, j: (i, 0))],
        out_specs=pl.BlockSpec((tm, d), lambda i, j: (i, 0)),
        scratch_shapes=[pltpu.VMEM((d, tm), f32), pltpu.VMEM((te, tm), bf16), pltpu.VMEM((te, tm), bf16)],
        compiler_params=_cparams(("parallel", "arbitrary")),
        name="peer_dense",
    )(xnt, u, vt, rank2, n1, c1, e2, h)


ATTN_BLOCK = 128


def _attn_band_kernel(*refs, dil):
    nq = Q_PER_KV * KV_PER_BRANCH * HEAD_DIM // LANES
    q_refs = refs[:nq]
    kc_ref, kp_ref, vc_ref, vp_ref, o_ref, l_ref, o_scr, l_scr = refs[nq:]
    i = pl.program_id(1)
    blk = ATTN_BLOCK
    scale = HEAD_DIM ** -0.5
    rows = Q_PER_KV * blk
    a_i = lax.broadcasted_iota(jnp.int32, (rows, blk), 0) % blk
    c_i = lax.broadcasted_iota(jnp.int32, (rows, blk), 1)
    mask_c = c_i <= a_i
    mask_p = jnp.logical_and(c_i >= a_i, i > 0)

    def residue(r, carry):
        sel = pl.ds(r, blk, stride=dil) if dil > 1 else pl.ds(0, blk)
        q = jnp.concatenate([qr[sel, :] for qr in q_refs], axis=1) * scale
        kc, kp, vc, vp = kc_ref[sel, :], kp_ref[sel, :], vc_ref[sel, :], vp_ref[sel, :]
        outs, lses = [], []
        for j in range(KV_PER_BRANCH):
            hs = slice(j * HEAD_DIM, (j + 1) * HEAD_DIM)
            qs = jnp.concatenate([q[:, (j * Q_PER_KV + x) * HEAD_DIM:(j * Q_PER_KV + x + 1) * HEAD_DIM]
                                  for x in range(Q_PER_KV)], axis=0).astype(bf16)
            s_c = jnp.where(mask_c, _dot_nt(qs, kc[:, hs].astype(bf16)), NEG_INF)
            s_p = jnp.where(mask_p, _dot_nt(qs, kp[:, hs].astype(bf16)), NEG_INF)
            m = jnp.maximum(jnp.max(s_c, axis=-1, keepdims=True), jnp.max(s_p, axis=-1, keepdims=True))
            p_c = jnp.exp(s_c - m)
            p_p = jnp.exp(s_p - m)
            l = jnp.sum(p_c, axis=-1, keepdims=True) + jnp.sum(p_p, axis=-1, keepdims=True)
            o = jnp.dot(p_c.astype(bf16), vc[:, hs].astype(bf16), preferred_element_type=f32)
            o = o + jnp.dot(p_p.astype(bf16), vp[:, hs].astype(bf16), preferred_element_type=f32)
            o = o / l
            lse = jnp.broadcast_to(m + jnp.log(l), (rows, HEAD_DIM))
            for x in range(Q_PER_KV):
                outs.append(o[x * blk:(x + 1) * blk])
                lses.append(lse[x * blk:(x + 1) * blk])
        per = LANES // HEAD_DIM
        for cb in range(nq):
            o_scr[cb, sel, :] = jnp.concatenate(outs[cb * per:(cb + 1) * per], axis=1)
            l_scr[cb, sel, :] = jnp.concatenate(lses[cb * per:(cb + 1) * per], axis=1)
        return carry

    lax.fori_loop(0, dil, residue, 0)
    for cb in range(nq):
        o_ref[:, cb * LANES:(cb + 1) * LANES] = o_scr[cb]
        l_ref[:, cb * LANES:(cb + 1) * LANES] = l_scr[cb]


def attn_band(q, k, v, g, dil):
    b, t, _ = q.shape
    span = ATTN_BLOCK * dil
    assert t % span == 0
    hq = Q_PER_KV * KV_PER_BRANCH * HEAD_DIM
    hk = KV_PER_BRANCH * HEAD_DIM
    cur = lambda bi, i: (bi, i, g)
    prev = lambda bi, i: (bi, jnp.maximum(i - 1, 0), g)
    osd = jax.ShapeDtypeStruct((b, t, hq), f32)
    nq = hq // LANES
    q_specs = [pl.BlockSpec((None, span, LANES), functools.partial(lambda bi, i, cb: (bi, i, g * nq + cb), cb=cb))
               for cb in range(nq)]
    return pl.pallas_call(
        functools.partial(_attn_band_kernel, dil=dil),
        out_shape=(osd, osd),
        grid=(b, t // span),
        in_specs=q_specs + [pl.BlockSpec((None, span, hk), cur), pl.BlockSpec((None, span, hk), prev),
                            pl.BlockSpec((None, span, hk), cur), pl.BlockSpec((None, span, hk), prev)],
        out_specs=(pl.BlockSpec((None, span, hq), lambda bi, i: (bi, i, 0)),
                   pl.BlockSpec((None, span, hq), lambda bi, i: (bi, i, 0))),
        scratch_shapes=[pltpu.VMEM((nq, span, LANES), f32), pltpu.VMEM((nq, span, LANES), f32)],
        compiler_params=_cparams(("parallel", "arbitrary")),
        name=f"attn_band_d{dil}",
    )(*([q] * nq), k, k, v, v)


def _merge3_kernel(o0, o1, o2, l0, l1, l2, out_ref):
    a0, a1, a2 = l0[...], l1[...], l2[...]
    m = jnp.maximum(jnp.maximum(a0, a1), a2)
    w0, w1, w2 = jnp.exp(a0 - m), jnp.exp(a1 - m), jnp.exp(a2 - m)
    out_ref[...] = ((w0 * o0[...] + w1 * o1[...] + w2 * o2[...]) / (w0 + w1 + w2)).astype(out_ref.dtype)


def merge3(os_, ls_, *, tm):
    n, c = os_[0].shape
    spec = pl.BlockSpec((tm, c), lambda i: (i, 0))
    return pl.pallas_call(
        _merge3_kernel,
        out_shape=jax.ShapeDtypeStruct((n, c), bf16),
        grid=(n // tm,),
        in_specs=[spec] * 6,
        out_specs=spec,
        compiler_params=_cparams(("parallel",)),
        name="attn_merge",
    )(*os_, *ls_)


def _attn_sample_kernel(q_ref, ck_ref, cv_ref, nk_ref, nv_ref, o_ref, *, t_new, win_buf):
    scale = HEAD_DIM ** -0.5
    q = q_ref[...] * scale
    n_cache = ck_ref.shape[0]
    n_new = nk_ref.shape[0]
    rows = Q_PER_KV * t_new
    t_c = lax.broadcasted_iota(jnp.int32, (rows, n_cache), 0) % t_new
    d_c = win_buf + t_c - lax.broadcasted_iota(jnp.int32, (rows, n_cache), 1)
    t_n = lax.broadcasted_iota(jnp.int32, (rows, n_new), 0) % t_new
    d_n = t_n - lax.broadcasted_iota(jnp.int32, (rows, n_new), 1)
    outs = [None] * (KV_PER_BRANCH * Q_PER_KV)
    for j in range(KV_PER_BRANCH):
        parts = []
        for g, (win, dil) in enumerate(DIL_PATTERNS):
            hq = (g * KV_PER_BRANCH + j) * Q_PER_KV
            hk = slice((g * KV_PER_BRANCH + j) * HEAD_DIM, (g * KV_PER_BRANCH + j + 1) * HEAD_DIM)
            qs = jnp.concatenate([q[:, (hq + r) * HEAD_DIM:(hq + r + 1) * HEAD_DIM] for r in range(Q_PER_KV)],
                                 axis=0).astype(bf16)
            ok_c = (d_c >= 0) & (d_c <= win) & ((d_c & (dil - 1)) == 0)
            ok_n = (d_n >= 0) & (d_n <= win) & ((d_n & (dil - 1)) == 0)
            s_c = jnp.where(ok_c, _dot_nt(qs, ck_ref[:, hk].astype(bf16)), NEG_INF)
            s_n = jnp.where(ok_n, _dot_nt(qs, nk_ref[:, hk].astype(bf16)), NEG_INF)
            parts.append((s_c, s_n, hk))
        m = None
        for s_c, s_n, _ in parts:
            mm = jnp.maximum(jnp.max(s_c, axis=-1, keepdims=True), jnp.max(s_n, axis=-1, keepdims=True))
            m = mm if m is None else jnp.maximum(m, mm)
        l = jnp.zeros_like(m)
        o = jnp.zeros((rows, HEAD_DIM), f32)
        for s_c, s_n, hk in parts:
            p_c = jnp.exp(s_c - m)
            p_n = jnp.exp(s_n - m)
            l = l + jnp.sum(p_c, axis=-1, keepdims=True) + jnp.sum(p_n, axis=-1, keepdims=True)
            o = o + jnp.dot(p_c.astype(bf16), cv_ref[:, hk].astype(bf16), preferred_element_type=f32)
            o = o + jnp.dot(p_n.astype(bf16), nv_ref[:, hk].astype(bf16), preferred_element_type=f32)
        o = o / l
        for r in range(Q_PER_KV):
            outs[j * Q_PER_KV + r] = o[r * t_new:(r + 1) * t_new]
    o_ref[...] = jnp.concatenate(outs, axis=1).astype(o_ref.dtype)


def attn_sample(q, ck, cv, nk, nv):
    b, t_new, qc = q.shape
    w = ck.shape[1]
    kc = ck.shape[2]
    n_new = nk.shape[1]
    return pl.pallas_call(
        functools.partial(_attn_sample_kernel, t_new=t_new, win_buf=w),
        out_shape=jax.ShapeDtypeStruct((b, t_new, ATTN_OUT_DIM), bf16),
        grid=(b,),
        in_specs=[pl.BlockSpec((None, t_new, qc), lambda i: (i, 0, 0)),
                  pl.BlockSpec((None, w, kc), lambda i: (i, 0, 0)),
                  pl.BlockSpec((None, w, kc), lambda i: (i, 0, 0)),
                  pl.BlockSpec((None, n_new, kc), lambda i: (i, 0, 0)),
                  pl.BlockSpec((None, n_new, kc), lambda i: (i, 0, 0))],
        out_specs=pl.BlockSpec((None, t_new, ATTN_OUT_DIM), lambda i: (i, 0, 0)),
        compiler_params=_cparams(("parallel",)),
        name="attn_sample",
    )(q, ck, cv, nk, nv)


def _prep_weights(norm_mix, norm_ffn, norm_ple, ssm_w_in, ssm_conv_w, ssm_conv_b, ssm_dt_bias, ssm_a_log, ssm_d,
                  ssm_norm, ssm_w_out, kv_norm, w_kv, k_norm, w_q, q_norm, w_o,
                  peer_w_query, peer_sub_keys, peer_u, peer_v, ple_w_proj, ple_w_gate):
    zx = D_INNER + CONV_DIM
    w_in = ssm_w_in[0]
    kdim = N_KV_HEADS * HEAD_DIM
    rep = LANES // HEAD_DIM
    return dict(
        norm_mix=norm_mix, norm_ffn=norm_ffn, norm_ple=norm_ple,
        w_z=w_in[:, :D_INNER].astype(bf16), w_xbc=w_in[:, D_INNER:zx].astype(bf16),
        w_dt=jnp.pad(w_in[:, zx:], ((0, 0), (0, LANES - SSM_HEADS))).astype(bf16),
        conv_w=ssm_conv_w[0], conv_b=ssm_conv_b[0], dt_bias=ssm_dt_bias[0], a_log=ssm_a_log[0],
        d_skip=ssm_d[0], ssm_norm=ssm_norm[0], w_out=ssm_w_out[0].astype(bf16),
        kv_norm=kv_norm, w_k=w_kv[:, :kdim].astype(bf16), w_v=w_kv[:, kdim:].astype(bf16),
        k_gain=jnp.tile(k_norm.astype(f32), rep).reshape(1, LANES),
        w_q=w_q[0].astype(bf16), q_gain=jnp.tile(q_norm[0].astype(f32), rep).reshape(1, LANES),
        w_o=w_o[0].astype(bf16),
        peer_wq=[peer_w_query[i].astype(bf16) for i in range(DEPTH)],
        peer_sk=[peer_sub_keys[i].astype(bf16) for i in range(DEPTH)],
        peer_u=[peer_u[i].astype(bf16) for i in range(DEPTH)],
        peer_vt=[peer_v[i].astype(bf16).T for i in range(DEPTH)],
        ple_wp=[ple_w_proj[i].astype(bf16) for i in range(DEPTH)],
        ple_wg=[ple_w_gate[i].astype(bf16) for i in range(DEPTH)],
    )


PEER_TOKEN_TILE = 512
PEER_SCORE_TILE = 256
PEER_EXPERT_TILE = 2 * SUBLANES * PEER_N_KEYS
PEER_SUB_TILE = PEER_EXPERT_TILE


def _peer_ple(h, p, i, wts, tm):
    n = h.shape[0]
    tm_d = PEER_TOKEN_TILE if n % PEER_TOKEN_TILE == 0 else n
    xnt, rank2, n1, c1, e2 = peer_scores(h, wts["norm_ffn"][i], wts["peer_wq"][i], wts["peer_sk"][i],
                                         tm=PEER_SCORE_TILE)
    h = peer_dense(xnt, wts["peer_u"][i], wts["peer_vt"][i], rank2, n1, c1, e2, h,
                   tm=tm_d, te=PEER_EXPERT_TILE, sub=PEER_SUB_TILE)
    return fused_matmul(h, wts["ple_wg"][i], gain=wts["norm_ple"][i], mode="ple", res=h,
                        ple=(p, wts["ple_wp"][i]), tm=tm)


def _trunk(x, p, conv_state, ssm_state, past_k, past_v, pos0, wts):
    b, t, d = x.shape
    n = b * t
    prompt = past_k is None
    tm = _token_tile(n)
    h = x.reshape(n, d)

    z = fused_matmul(h, wts["w_z"], gain=wts["norm_mix"][0], tm=tm).reshape(b, t, D_INNER)
    xbc = fused_matmul(h, wts["w_xbc"], gain=wts["norm_mix"][0], tm=tm).reshape(b, t, CONV_DIM)
    dtr = fused_matmul(h, wts["w_dt"], gain=wts["norm_mix"][0], tm=tm)
    new_conv = xbc[:, t - (CONV_W - 1):, :] if t >= CONV_W - 1 else None
    dtr = dtr.reshape(b, t, LANES)
    tp = -(-t // SSD_CHUNK) * SSD_CHUNK
    if tp != t:
        padt = ((0, 0), (0, tp - t), (0, 0))
        z_p, xbc_p, dtr_p = jnp.pad(z, padt), jnp.pad(xbc, padt), jnp.pad(dtr, padt)
    else:
        z_p, xbc_p, dtr_p = z, xbc, dtr
    cb0 = jnp.pad(conv_state.astype(f32), ((0, 0), (SUBLANES - (CONV_W - 1), 0), (0, 0)))
    h0 = ssm_state.astype(f32).reshape(b, SSM_HEADS * SSM_HEAD_DIM, D_STATE)
    y, h_fin = ssd_mixer(z_p, xbc_p, dtr_p, h0, cb0, wts["conv_w"], wts["conv_b"], wts["dt_bias"], wts["a_log"],
                         wts["d_skip"], wts["ssm_norm"], t_valid=(None if tp == t else t))
    y = y[:, :t].reshape(n, D_INNER)
    h = fused_matmul(y, wts["w_out"], mode="add", res=h, tm=tm)
    new_ssm = h_fin.reshape(b, SSM_HEADS, SSM_HEAD_DIM, D_STATE)
    h = _peer_ple(h, p[0].reshape(n, -1), 0, wts, tm)

    pos = pos0 + jnp.arange(t, dtype=jnp.int32)
    tabs = _rope_tables(pos)
    if t < tm:
        tabs = tuple(jnp.tile(a, (tm // t, 1)) for a in tabs)
    k_new = fused_matmul(h, wts["w_k"], gain=wts["kv_norm"], mode="headrope",
                         headrope=(wts["k_gain"],) + tabs, tm=tm)
    v_new = fused_matmul(h, wts["w_v"], gain=wts["kv_norm"], tm=tm)

    q = fused_matmul(h, wts["w_q"], gain=wts["norm_mix"][1], mode="headrope",
                     headrope=(wts["q_gain"],) + tabs, tm=tm)
    kdim = N_KV_HEADS * HEAD_DIM
    if prompt:
        q3, k3, v3 = q.reshape(b, t, -1), k_new.reshape(b, t, kdim), v_new.reshape(b, t, kdim)
        os_, ls_ = [], []
        for g, (_, dil) in enumerate(DIL_PATTERNS):
            o_g, l_g = attn_band(q3, k3, v3, g, dil)
            os_.append(o_g.reshape(n, ATTN_OUT_DIM))
            ls_.append(l_g.reshape(n, ATTN_OUT_DIM))
        att = merge3(os_, ls_, tm=tm)
    else:
        padn = ((0, 0), (0, SSD_CHUNK - t), (0, 0))
        nk = jnp.pad(k_new.reshape(b, t, kdim), padn)
        nv = jnp.pad(v_new.reshape(b, t, kdim), padn)
        w = past_k.shape[1]
        att = attn_sample(q.reshape(b, t, -1), past_k.reshape(b, w, kdim).astype(f32),
                          past_v.reshape(b, w, kdim).astype(f32), nk, nv).reshape(n, ATTN_OUT_DIM)
    h = fused_matmul(att, wts["w_o"], mode="add", res=h, tm=tm)
    h = _peer_ple(h, p[1].reshape(n, -1), 1, wts, tm)

    return (h.reshape(b, t, d), new_conv[None], new_ssm[None],
            k_new.reshape(b, t, N_KV_HEADS, HEAD_DIM), v_new.reshape(b, t, N_KV_HEADS, HEAD_DIM))


def kernel(x_prompt, x_sample, state_conv, state_ssm, cache_k, cache_v, p_prompt, p_sample, norm_mix, norm_ffn, norm_ple, ssm_w_in, ssm_conv_w, ssm_conv_b, ssm_dt_bias, ssm_a_log, ssm_d, ssm_norm, ssm_w_out, kv_norm, w_kv, k_norm, w_q, q_norm, w_o, peer_w_query, peer_sub_keys, peer_u, peer_v, ple_w_proj, ple_w_gate):
    wts = _prep_weights(norm_mix, norm_ffn, norm_ple, ssm_w_in, ssm_conv_w, ssm_conv_b, ssm_dt_bias, ssm_a_log,
                        ssm_d, ssm_norm, ssm_w_out, kv_norm, w_kv, k_norm, w_q, q_norm, w_o,
                        peer_w_query, peer_sub_keys, peer_u, peer_v, ple_w_proj, ple_w_gate)
    b_p, t_p, _ = x_prompt.shape
    zero_conv = jnp.zeros((b_p, CONV_W - 1, CONV_DIM), x_prompt.dtype)
    zero_ssm = jnp.zeros((b_p, SSM_HEADS, SSM_HEAD_DIM, D_STATE), x_prompt.dtype)
    y_p, conv_p, ssm_p, k_p, v_p = _trunk(x_prompt, p_prompt, zero_conv, zero_ssm, None, None, 0, wts)
    keep = min(max(w for w, _ in DIL_PATTERNS), t_p)
    past_len = 16384
    y_s, conv_s, ssm_s, k_s, v_s = _trunk(x_sample, p_sample, state_conv[0], state_ssm[0], cache_k, cache_v,
                                          past_len, wts)
    if keep < t_p:
        k_p, v_p = k_p[:, -keep:], v_p[:, -keep:]
    return (y_p, y_s, conv_p, ssm_p, k_p, v_p, conv_s, ssm_s, k_s, v_s)
```

```python
import functools
import math

import jax
import jax.numpy as jnp
from jax import lax
from jax.experimental import pallas as pl
from jax.experimental.pallas import tpu as pltpu

f32 = jnp.float32
bf16 = jnp.bfloat16

D_MODEL = 1024
DEPTH = 2
N_A_LAYERS = 1
D_INNER = 2048
SSM_HEAD_DIM = 64
SSM_HEADS = 32
SSM_GROUPS = 8
SSM_HEADS_PER_GROUP = 4
D_STATE = 128
CONV_W = 4
CONV_DIM = 4096
SSD_CHUNK = 128
HEAD_DIM = 64
DIL_PATTERNS = ((128, 1), (512, 4), (2048, 16))
N_DIL = 3
KV_PER_BRANCH = 2
Q_PER_KV = 4
N_Q_HEADS = 24
N_KV_HEADS = 6
ATTN_OUT_DIM = 512
ROPE_DIM = 16
ROPE_THETA = 500000.0
PEER_HEADS = 8
PEER_N_KEYS = 128
PEER_TOPK = 16
PEER_HALF = 128
EPS = 1e-6

LANES = 128
SUBLANES = 8
VMEM_LIMIT_BYTES = 56 * 1024 * 1024

NEG_INF = float("-inf")


def _cparams(sem):
    return pltpu.CompilerParams(dimension_semantics=sem, vmem_limit_bytes=VMEM_LIMIT_BYTES)


def _dot_nt(a, b):
    return lax.dot_general(a, b, (((1,), (1,)), ((), ())), preferred_element_type=f32)


def _dot_tn(a, b):
    return lax.dot_general(a, b, (((0,), (0,)), ((), ())), preferred_element_type=f32)


def _sigmoid(x):
    return 1.0 / (1.0 + jnp.exp(-x))


def _silu(x):
    return x * _sigmoid(x)


def _softplus(x):
    return jnp.maximum(x, 0.0) + jnp.log1p(jnp.exp(-jnp.abs(x)))


def _gelu_tanh(x):
    c = math.sqrt(2.0 / math.pi)
    return x * (0.5 + 0.5 * jnp.tanh(x * (c + (c * 0.044715) * (x * x))))


def _fmm_kernel(*refs, has_norm, mode):
    it = iter(refs)
    x_ref = next(it)
    g_ref = next(it) if has_norm else None
    w_ref = next(it)
    res_ref = next(it) if mode in ("add", "ple") else None
    if mode == "ple":
        p_ref, wp_ref = next(it), next(it)
    if mode == "headrope":
        hg_ref, ta_ref, tb_ref, tc_ref, bd_ref = next(it), next(it), next(it), next(it), next(it)
    o_ref = next(it)
    xn_ref = next(it)

    @pl.when(pl.program_id(1) == 0)
    def _():
        x = x_ref[...].astype(f32)
        if has_norm:
            ms = jnp.mean(x * x, axis=-1, keepdims=True)
            x = x * lax.rsqrt(ms + EPS) * g_ref[...]
        xn_ref[...] = x.astype(bf16)

    acc = jnp.dot(xn_ref[...], w_ref[...], preferred_element_type=f32)
    if mode == "plain":
        o_ref[...] = acc.astype(o_ref.dtype)
    elif mode == "add":
        o_ref[...] = res_ref[...] + acc
    elif mode == "ple":
        pp = jnp.dot(p_ref[...].astype(bf16), wp_ref[...], preferred_element_type=f32)
        o_ref[...] = res_ref[...] + pp * _sigmoid(acc)
    elif mode == "headrope":
        tn = acc.shape[1]
        ta, tb, tc = ta_ref[...], tb_ref[...], tc_ref[...]
        hg = hg_ref[...]
        bd = bd_ref[...]
        for nb in range(tn // LANES):
            blk = acc[:, nb * LANES:(nb + 1) * LANES]
            ms = jnp.dot(blk * blk, bd, precision=lax.Precision.HIGHEST, preferred_element_type=f32)
            y = blk * lax.rsqrt(ms + EPS) * hg
            y = y * ta + pltpu.roll(y, LANES - ROPE_DIM // 2, 1) * tb + pltpu.roll(y, ROPE_DIM // 2, 1) * tc
            o_ref[:, nb * LANES:(nb + 1) * LANES] = y


MATMUL_VMEM_BUDGET = 40 * 1024 * 1024


def _token_tile(m):
    return next((c for c in (1024, 512, 256, 128) if m % c == 0), m)


def _col_tile(tm, k, n, x_bytes, mode, kp):
    out_blocks = 2 if mode in ("add", "ple") else 1
    for tn in sorted({n, 2048, 1536, 1024, 768, 512, 384, 256, 128}, reverse=True):
        if tn > n or n % tn:
            continue
        use = 2 * tm * k * x_bytes + tm * k * 2 + 2 * k * tn * 2 + 2 * out_blocks * tm * tn * 4
        if mode == "ple":
            use += 2 * tm * kp * 4 + 2 * kp * tn * 2
        if use <= MATMUL_VMEM_BUDGET:
            return tn
    raise ValueError("no column tile fits")


def fused_matmul(x, w, *, gain=None, mode="plain", res=None, ple=None, headrope=None, tm, tn=None, out_dtype=f32):
    m, k = x.shape
    n = w.shape[1]
    if tn is None:
        tn = _col_tile(tm, k, n, x.dtype.itemsize, mode, ple[0].shape[1] if ple else 0)
    assert m % tm == 0 and n % tn == 0, (m, tm, n, tn)
    has_norm = gain is not None
    args = [x]
    in_specs = [pl.BlockSpec((tm, k), lambda i, j: (i, 0))]
    if has_norm:
        args.append(gain.reshape(1, k).astype(f32))
        in_specs.append(pl.BlockSpec((1, k), lambda i, j: (0, 0)))
    args.append(w)
    in_specs.append(pl.BlockSpec((k, tn), lambda i, j: (0, j)))
    if mode in ("add", "ple"):
        args.append(res)
        in_specs.append(pl.BlockSpec((tm, tn), lambda i, j: (i, j)))
    if mode == "ple":
        p, wp = ple
        kp = p.shape[1]
        args += [p, wp]
        in_specs += [pl.BlockSpec((tm, kp), lambda i, j: (i, 0)), pl.BlockSpec((kp, tn), lambda i, j: (0, j))]
    if mode == "headrope":
        hg, ta, tb, tc = headrope
        period = ta.shape[0] // tm
        args += [hg, ta, tb, tc, _head_mean_matrix()]
        in_specs.append(pl.BlockSpec((1, LANES), lambda i, j: (0, 0)))
        for _ in range(3):
            in_specs.append(pl.BlockSpec((tm, LANES), lambda i, j: (i % period, 0)))
        in_specs.append(pl.BlockSpec((LANES, LANES), lambda i, j: (0, 0)))
    return pl.pallas_call(
        functools.partial(_fmm_kernel, has_norm=has_norm, mode=mode),
        out_shape=jax.ShapeDtypeStruct((m, n), out_dtype),
        grid=(m // tm, n // tn),
        in_specs=in_specs,
        out_specs=pl.BlockSpec((tm, tn), lambda i, j: (i, j)),
        scratch_shapes=[pltpu.VMEM((tm, k), bf16)],
        compiler_params=_cparams(("parallel", "arbitrary")),
        name="fused_matmul_" + mode,
    )(*args)


def _head_mean_matrix():
    r = jnp.arange(LANES) // HEAD_DIM
    return (r[:, None] == r[None, :]).astype(f32) / HEAD_DIM


def _rope_tables(pos):
    half = ROPE_DIM // 2
    inv = ROPE_THETA ** (-jnp.arange(0, ROPE_DIM, 2, dtype=f32) / ROPE_DIM)
    ang = pos.astype(f32)[:, None] * inv[None, :]
    cos, sin = jnp.cos(ang), jnp.sin(ang)
    t = pos.shape[0]
    ones = jnp.ones((t, HEAD_DIM - ROPE_DIM), f32)
    zeros_h = jnp.zeros((t, half), f32)
    zeros_r = jnp.zeros((t, HEAD_DIM - ROPE_DIM), f32)
    ta = jnp.concatenate([cos, cos, ones], axis=1)
    tb = jnp.concatenate([-sin, zeros_h, zeros_r], axis=1)
    tc = jnp.concatenate([zeros_h, sin, zeros_r], axis=1)
    rep = LANES // HEAD_DIM
    return tuple(jnp.tile(a, (1, rep)) for a in (ta, tb, tc))


def _ssd_kernel(z_ref, xbc_ref, dt_ref, h0_ref, cb0_ref, cw_ref, cbias_ref, dtb_ref, alog_ref, dsk_ref, ng_ref,
                ex_ref, y_ref, hout_ref, h_scr, tail_scr, xc_scr, *, chunk, t_valid):
    c = pl.program_id(1)
    nc = pl.num_programs(1)
    L = chunk
    hp = SSM_HEADS_PER_GROUP * SSM_HEAD_DIM

    @pl.when(c == 0)
    def _():
        for g in range(SSM_GROUPS):
            h_scr[g] = h0_ref[g * hp:(g + 1) * hp, :].T
        tail_scr[0:SUBLANES, :] = cb0_ref[...]

    w = cw_ref[...]
    tail_scr[SUBLANES:2 * SUBLANES, :] = xbc_ref[0:SUBLANES, :]
    for lo, hi, src, off in ((0, SUBLANES, tail_scr, SUBLANES), (SUBLANES, L, xbc_ref, 0)):
        acc = src[lo + off:hi + off, :] * w[CONV_W - 1:CONV_W, :] + cbias_ref[...]
        for s in range(1, CONV_W):
            acc = acc + src[lo + off - s:hi + off - s, :] * w[CONV_W - 1 - s:CONV_W - s, :]
        xc_scr[lo:hi, :] = _silu(acc)
    tail_scr[0:SUBLANES, :] = xbc_ref[L - SUBLANES:L, :]

    dt = _softplus(dt_ref[...] + dtb_ref[...])
    if t_valid is not None:
        rowid = c * L + lax.broadcasted_iota(jnp.int32, (L, LANES), 0)
        dt = jnp.where(rowid < t_valid, dt, 0.0)
    a = -jnp.exp(alog_ref[...])
    da = dt * a
    r_i = lax.broadcasted_iota(jnp.int32, (L, L), 0)
    c_i = lax.broadcasted_iota(jnp.int32, (L, L), 1)
    causal = r_i >= c_i
    cum = jnp.dot(causal.astype(f32), da, precision=lax.Precision.HIGHEST, preferred_element_type=f32)
    cum_t = cum.T
    cum_last = cum[L - 1:L, :]
    tailw = jnp.exp(cum_last - cum) * dt
    ecum = jnp.exp(cum)

    src = jnp.concatenate([dt, tailw, ecum], axis=0)
    src_hi = src.astype(bf16)
    src_lo = (src - src_hi.astype(f32)).astype(bf16)
    pieces = jnp.concatenate([src_hi, src_lo], axis=0)

    for g in range(SSM_GROUPS):
        gs = slice(g * hp, (g + 1) * hp)
        ex = jnp.dot(pieces, ex_ref[:, gs], preferred_element_type=f32)
        dt_x = ex[0:L] + ex[3 * L:4 * L]
        tw_x = ex[L:2 * L] + ex[4 * L:5 * L]
        ec_x = ex[2 * L:3 * L] + ex[5 * L:6 * L]
        x_g = xc_scr[:, gs]
        b_g = xc_scr[:, D_INNER + g * D_STATE:D_INNER + (g + 1) * D_STATE].astype(bf16)
        c_g = xc_scr[:, D_INNER + SSM_GROUPS * D_STATE + g * D_STATE:
                     D_INNER + SSM_GROUPS * D_STATE + (g + 1) * D_STATE].astype(bf16)
        cb = _dot_nt(c_g, b_g)
        ht_g = h_scr[g]
        y_in = jnp.dot(c_g, ht_g.astype(bf16), preferred_element_type=f32)
        xdt = (x_g * dt_x).astype(bf16)
        ys = []
        for hh in range(SSM_HEADS_PER_GROUP):
            h = g * SSM_HEADS_PER_GROUP + hh
            seg = cum[:, h:h + 1] - cum_t[h:h + 1, :]
            dec = jnp.exp(jnp.where(causal, seg, NEG_INF))
            m = (cb * dec).astype(bf16)
            ys.append(jnp.dot(m, xdt[:, hh * SSM_HEAD_DIM:(hh + 1) * SSM_HEAD_DIM], preferred_element_type=f32))
        yg = jnp.concatenate(ys, axis=1) + y_in * ec_x + x_g * dsk_ref[:, gs]
        upd = _dot_tn(b_g, (x_g * tw_x).astype(bf16))
        h_scr[g] = ht_g * ec_x[L - 1:L, :] + upd
        yg = yg * _silu(z_ref[:, gs])
        yg = yg * lax.rsqrt(jnp.mean(yg * yg, axis=-1, keepdims=True) + EPS)
        y_ref[:, gs] = (yg * ng_ref[:, gs]).astype(y_ref.dtype)

    @pl.when(c == nc - 1)
    def _():
        for g in range(SSM_GROUPS):
            hout_ref[g * hp:(g + 1) * hp, :] = h_scr[g].T


def ssd_mixer(z, xbc, dtr, h0, cb0, conv_w, conv_b, dt_bias, a_log, d_skip, norm_g, *, t_valid):
    b, t, _ = z.shape
    L = SSD_CHUNK
    assert t % L == 0
    nc = t // L
    hp_all = SSM_HEADS * SSM_HEAD_DIM
    pad_h = LANES - SSM_HEADS

    def lane_pad(v):
        return jnp.pad(v.astype(f32), (0, pad_h)).reshape(1, LANES)

    cw = jnp.pad(conv_w.astype(f32), ((0, SUBLANES - CONV_W), (0, 0)))
    head_of_col = jnp.arange(hp_all) // SSM_HEAD_DIM
    expand = (jnp.arange(LANES)[:, None] == head_of_col[None, :]).astype(bf16)
    dsk_cols = jnp.repeat(d_skip.astype(f32), SSM_HEAD_DIM).reshape(1, hp_all)
    const = lambda shape: pl.BlockSpec(shape, lambda i, j: (0,) * len(shape))
    return pl.pallas_call(
        functools.partial(_ssd_kernel, chunk=L, t_valid=t_valid),
        out_shape=(jax.ShapeDtypeStruct((b, t, D_INNER), bf16),
                   jax.ShapeDtypeStruct((b, hp_all, D_STATE), f32)),
        grid=(b, nc),
        in_specs=[
            pl.BlockSpec((None, L, D_INNER), lambda i, j: (i, j, 0)),
            pl.BlockSpec((None, L, CONV_DIM), lambda i, j: (i, j, 0)),
            pl.BlockSpec((None, L, LANES), lambda i, j: (i, j, 0)),
            pl.BlockSpec((None, hp_all, D_STATE), lambda i, j: (i, 0, 0)),
            pl.BlockSpec((None, SUBLANES, CONV_DIM), lambda i, j: (i, 0, 0)),
            const((SUBLANES, CONV_DIM)),
            const((1, CONV_DIM)),
            const((1, LANES)), const((1, LANES)), const((1, D_INNER)),
            const((1, D_INNER)),
            const((LANES, hp_all)),
        ],
        out_specs=(pl.BlockSpec((None, L, D_INNER), lambda i, j: (i, j, 0)),
                   pl.BlockSpec((None, hp_all, D_STATE), lambda i, j: (i, 0, 0))),
        scratch_shapes=[pltpu.VMEM((SSM_GROUPS, D_STATE, hp_all // SSM_GROUPS), f32),
                        pltpu.VMEM((2 * SUBLANES, CONV_DIM), f32),
                        pltpu.VMEM((L, CONV_DIM), f32)],
        compiler_params=_cparams(("parallel", "arbitrary")),
        name="ssd_mixer",
    )(z, xbc, dtr, h0, cb0, cw, conv_b.reshape(1, CONV_DIM).astype(f32),
      lane_pad(dt_bias), lane_pad(a_log), dsk_cols, norm_g.reshape(1, D_INNER).astype(f32), expand)


def _sort_network(n):
    def merge(lo, hi, r):
        step = r * 2
        if step < hi - lo:
            yield from merge(lo, hi, step)
            yield from merge(lo + r, hi, step)
            yield from [(i, i + r) for i in range(lo + r, hi - r, step)]
        else:
            yield (lo, lo + r)

    def sort(lo, hi):
        if hi - lo >= 1:
            mid = lo + (hi - lo) // 2
            yield from sort(lo, mid)
            yield from sort(mid + 1, hi)
            yield from merge(lo, hi, 1)

    return list(sort(0, n - 1))


def _sublane_max(x):
    for shift in (4, 2, 1):
        x = jnp.maximum(x, pltpu.roll(x, shift, 0))
    return x


def _pop_columns(cols, extra, n):
    cols = list(cols)
    out = []
    for r in range(n):
        top = cols[0] if extra is None else jnp.maximum(cols[0], extra)
        m = _sublane_max(top)
        out.append(m)
        if r + 1 == n:
            break
        hit = cols[0] == m
        for i in range(min(n - r - 1, len(cols))):
            below = cols[i + 1] if i + 1 < len(cols) else NEG_INF
            cols[i] = jnp.where(hit, below, cols[i])
        if extra is not None:
            extra = jnp.where(extra == m, NEG_INF, extra)
    return out


def _top_values(s, n):
    k = s.shape[0] // SUBLANES
    v = [s[i * SUBLANES:(i + 1) * SUBLANES] for i in range(k)]
    for i, j in _sort_network(k):
        v[i], v[j] = jnp.maximum(v[i], v[j]), jnp.minimum(v[i], v[j])
    return _pop_columns(v[:n], None, n)


def _bf16_pair_words(x):
    hi = pltpu.bitcast(x.astype(bf16).astype(f32), jnp.uint32)
    return hi | (hi >> 16)


def _row_as_bf16_tile(words, rows, zero_words=None):
    tile = jnp.broadcast_to(words, (SUBLANES, words.shape[1]))
    if zero_words is not None:
        tile = tile + zero_words
    packed = pltpu.bitcast(tile, bf16)
    return jnp.concatenate([packed] * (rows // packed.shape[0]), axis=0)


def _peer_scores_kernel(h_ref, g_ref, wq_ref, sk_ref, xnt_ref, rank_ref, n1_ref, c1_ref, e2_ref, q_scr):
    x = h_ref[...]
    ms = jnp.mean(x * x, axis=-1, keepdims=True)
    xn = x * lax.rsqrt(ms + EPS) * g_ref[...]
    xnt_ref[...] = xn.T.astype(bf16)
    q = jnp.dot(xn.astype(bf16), wq_ref[...], preferred_element_type=f32).astype(bf16)
    for i in range(2 * PEER_HEADS):
        q_scr[i] = q[:, i * PEER_HALF:(i + 1) * PEER_HALF]
    kk = PEER_TOPK
    sk0 = sk_ref[0]
    sk1 = sk_ref[1]

    nkeys = sk0.shape[0]
    tokens = h_ref.shape[0]

    def rep(tile):
        return jnp.concatenate([tile] * (nkeys // SUBLANES), axis=0)

    def select_half(s1, s2):
        av = _top_values(s1, kk)
        bv = _top_values(s2, kk)
        sub = lax.broadcasted_iota(jnp.int32, av[0].shape, 0)
        a_lo, a_hi = av[SUBLANES - 1], av[2 * SUBLANES - 1]
        for r in range(SUBLANES - 2, -1, -1):
            a_lo = jnp.where(sub == r, av[r], a_lo)
            a_hi = jnp.where(sub == r, av[SUBLANES + r], a_hi)
        cand = [a_lo + bv[0]]
        for c in range(1, kk):
            cand.append(jnp.where(sub < kk // (c + 1), a_lo + bv[c], NEG_INF))
        best = _pop_columns(cand, a_hi + bv[0], kk)
        tau, top = best[kk - 1], best[0]
        zsum = jnp.zeros_like(top)
        for r in range(kk):
            zsum = zsum + jnp.exp(best[r] - top)
        tau_r = rep(tau)
        rank2 = jnp.full(s2.shape, float(kk), f32)
        n1 = jnp.zeros(s1.shape, f32)
        for c in range(kk - 1, -1, -1):
            b_c = rep(bv[c])
            rank2 = jnp.where(s2 >= b_c, float(c), rank2)
            n1 = n1 + jnp.where(s1 + b_c >= tau_r, 1.0, 0.0)
        c1 = jnp.exp(s1 - rep(av[0])) / rep(zsum)
        e2 = jnp.exp(s2 - rep(bv[0]))
        return rank2, n1, c1, e2

    def head(h, carry):
        s1 = _dot_nt(sk0, q_scr[2 * h])
        s2 = _dot_nt(sk1, q_scr[2 * h + 1])
        parts = [select_half(s1[:, l0:l0 + LANES], s2[:, l0:l0 + LANES]) for l0 in range(0, tokens, LANES)]
        rank2, n1, c1, e2 = (jnp.concatenate(p, axis=1) for p in zip(*parts))
        rank_ref[h] = rank2.astype(rank_ref.dtype)
        n1_ref[h] = _bf16_pair_words(n1)
        c1_ref[h] = _bf16_pair_words(c1)
        e2_ref[h] = e2.astype(e2_ref.dtype)
        return carry

    lax.fori_loop(0, PEER_HEADS, head, 0, unroll=True)


def peer_scores(h, gain, wq, sk, *, tm):
    n, d = h.shape
    nq = wq.shape[1]
    sel16 = jax.ShapeDtypeStruct((PEER_HEADS, PEER_N_KEYS, n), bf16)
    sel32 = jax.ShapeDtypeStruct((PEER_HEADS, PEER_N_KEYS, n), jnp.uint32)
    sel_spec = pl.BlockSpec((PEER_HEADS, PEER_N_KEYS, tm), lambda i: (0, 0, i))
    return pl.pallas_call(
        _peer_scores_kernel,
        out_shape=(jax.ShapeDtypeStruct((d, n), bf16), sel16, sel32, sel32, sel16),
        grid=(n // tm,),
        in_specs=[pl.BlockSpec((tm, d), lambda i: (i, 0)),
                  pl.BlockSpec((1, d), lambda i: (0, 0)),
                  pl.BlockSpec((d, nq), lambda i: (0, 0)),
                  pl.BlockSpec((2, PEER_N_KEYS, PEER_HALF), lambda i: (0, 0, 0))],
        out_specs=(pl.BlockSpec((d, tm), lambda i: (0, i)), sel_spec, sel_spec, sel_spec, sel_spec),
        scratch_shapes=[pltpu.VMEM((2 * PEER_HEADS, tm, PEER_HALF), bf16)],
        compiler_params=_cparams(("parallel",)),
        name="peer_scores",
    )(h, gain.reshape(1, d).astype(f32), wq, sk)


def _peer_dense_kernel(xnt_ref, u_ref, vt_ref, rank_ref, n1_ref, c1_ref, e2_ref, h_ref, o_ref,
                       acc_scr, coef_a, coef_b, *, te, sub, ne):
    j = pl.program_id(1)
    nk = PEER_N_KEYS
    a_per_step = te // nk
    a_per_sub = sub // nk
    tm = xnt_ref.shape[1]
    pk = 2 * SUBLANES
    zero = jnp.zeros((), bf16)

    @pl.when(j == 0)
    def _():
        acc_scr[...] = jnp.zeros_like(acc_scr)
        coef_b[...] = jnp.zeros_like(coef_b)

    def drain(src, part, parts):
        d = acc_scr.shape[0]
        rows = slice(part * d // parts, (part + 1) * d // parts)
        upd = jnp.dot(vt_ref[rows, :], src[...], preferred_element_type=f32)
        acc_scr[rows, :] += upd
        bits = pltpu.bitcast(upd[0:SUBLANES, :], jnp.uint32)
        return lax.shift_right_logical(bits, jnp.uint32(32))

    def build(dst, src):
        xnt = xnt_ref[...]
        nsub = te // sub

        def scores(s):
            return jnp.dot(u_ref[s * sub:(s + 1) * sub, :], xnt, preferred_element_type=f32)

        def masks(act, s, token):
            for ai in range(a_per_sub):
                ar = s * a_per_sub + ai
                grp = j * (a_per_step // SUBLANES) + ar // SUBLANES
                row = ar % SUBLANES
                wsum = None
                for h in range(PEER_HEADS):
                    tie = token if (h == 0 and token is not None) else None
                    n1a = _row_as_bf16_tile(n1_ref[h, grp, row:row + 1, :], nk, tie)
                    c1a = _row_as_bf16_tile(c1_ref[h, grp, row:row + 1, :], nk)
                    term = jnp.where(rank_ref[h] < n1a, e2_ref[h], zero) * c1a
                    wsum = term if wsum is None else wsum + term
                g = _gelu_tanh(act[ai * nk:(ai + 1) * nk].astype(bf16))
                dst[ar * nk:(ar + 1) * nk, :] = wsum * g

        token = None
        for s in range(nsub):
            act = scores(s)
            nxt = drain(src, s, nsub)
            masks(act, s, token)
            token = nxt

    @pl.when(jnp.logical_and(j < ne, j % 2 == 0))
    def _():
        build(coef_a, coef_b)

    @pl.when(jnp.logical_and(j < ne, j % 2 == 1))
    def _():
        build(coef_b, coef_a)

    @pl.when(j == ne)
    def _():
        drain(coef_b if ne % 2 == 0 else coef_a, 0, 1)
        o_ref[...] = h_ref[...] + acc_scr[...].T


def peer_dense(xnt, u, vt, rank2, n1, c1, e2, h, *, tm, te, sub):
    n, d = h.shape
    n_exp = u.shape[0]
    ne = n_exp // te
    assert (te // PEER_N_KEYS) % SUBLANES == 0
    sel_spec = pl.BlockSpec((PEER_HEADS, PEER_N_KEYS, tm), lambda i, j: (0, 0, i))
    ngrp = PEER_N_KEYS // SUBLANES
    n1 = n1.reshape(PEER_HEADS, ngrp, SUBLANES, n)
    c1 = c1.reshape(PEER_HEADS, ngrp, SUBLANES, n)
    row_spec = pl.BlockSpec((PEER_HEADS, ngrp, SUBLANES, tm), lambda i, j: (0, 0, 0, i))
    return pl.pallas_call(
        functools.partial(_peer_dense_kernel, te=te, sub=sub, ne=ne),
        out_shape=jax.ShapeDtypeStruct((n, d), f32),
        grid=(n // tm, ne + 1),
        in_specs=[pl.BlockSpec((d, tm), lambda i, j: (0, i)),
                  pl.BlockSpec((te, d), lambda i, j: (jnp.minimum(j, ne - 1), 0)),
                  pl.BlockSpec((d, te), lambda i, j: (0, jnp.maximum(j - 1, 0))),
                  sel_spec, row_spec, row_spec, sel_spec,
                  pl.BlockSpec((tm, d), lambda i, j: (i, 0))],
        out_specs=pl.BlockSpec((tm, d), lambda i, j: (i, 0)),
        scratch_shapes=[pltpu.VMEM((d, tm), f32), pltpu.VMEM((te, tm), bf16), pltpu.VMEM((te, tm), bf16)],
        compiler_params=_cparams(("parallel", "arbitrary")),
        name="peer_dense",
    )(xnt, u, vt, rank2, n1, c1, e2, h)


ATTN_BLOCK = 128


def _attn_band_kernel(*refs, dil):
    nq = Q_PER_KV * KV_PER_BRANCH * HEAD_DIM // LANES
    q_refs = refs[:nq]
    kc_ref, kp_ref, vc_ref, vp_ref, o_ref, l_ref, o_scr, l_scr = refs[nq:]
    i = pl.program_id(1)
    blk = ATTN_BLOCK
    scale = HEAD_DIM ** -0.5
    rows = Q_PER_KV * blk
    a_i = lax.broadcasted_iota(jnp.int32, (rows, blk), 0) % blk
    c_i = lax.broadcasted_iota(jnp.int32, (rows, blk), 1)
    mask_c = c_i <= a_i
    mask_p = jnp.logical_and(c_i >= a_i, i > 0)

    def residue(r, carry):
        sel = pl.ds(r, blk, stride=dil) if dil > 1 else pl.ds(0, blk)
        q = jnp.concatenate([qr[sel, :] for qr in q_refs], axis=1) * scale
        kc, kp, vc, vp = kc_ref[sel, :], kp_ref[sel, :], vc_ref[sel, :], vp_ref[sel, :]
        outs, lses = [], []
        for j in range(KV_PER_BRANCH):
            hs = slice(j * HEAD_DIM, (j + 1) * HEAD_DIM)
            qs = jnp.concatenate([q[:, (j * Q_PER_KV + x) * HEAD_DIM:(j * Q_PER_KV + x + 1) * HEAD_DIM]
                                  for x in range(Q_PER_KV)], axis=0).astype(bf16)
            s_c = jnp.where(mask_c, _dot_nt(qs, kc[:, hs].astype(bf16)), NEG_INF)
            s_p = jnp.where(mask_p, _dot_nt(qs, kp[:, hs].astype(bf16)), NEG_INF)
            m = jnp.maximum(jnp.max(s_c, axis=-1, keepdims=True), jnp.max(s_p, axis=-1, keepdims=True))
            p_c = jnp.exp(s_c - m)
            p_p = jnp.exp(s_p - m)
            l = jnp.sum(p_c, axis=-1, keepdims=True) + jnp.sum(p_p, axis=-1, keepdims=True)
            o = jnp.dot(p_c.astype(bf16), vc[:, hs].astype(bf16), preferred_element_type=f32)
            o = o + jnp.dot(p_p.astype(bf16), vp[:, hs].astype(bf16), preferred_element_type=f32)
            o = o / l
            lse = jnp.broadcast_to(m + jnp.log(l), (rows, HEAD_DIM))
            for x in range(Q_PER_KV):
                outs.append(o[x * blk:(x + 1) * blk])
                lses.append(lse[x * blk:(x + 1) * blk])
        per = LANES // HEAD_DIM
        for cb in range(nq):
            o_scr[cb, sel, :] = jnp.concatenate(outs[cb * per:(cb + 1) * per], axis=1)
            l_scr[cb, sel, :] = jnp.concatenate(lses[cb * per:(cb + 1) * per], axis=1)
        return carry

    lax.fori_loop(0, dil, residue, 0)
    for cb in range(nq):
        o_ref[:, cb * LANES:(cb + 1) * LANES] = o_scr[cb]
        l_ref[:, cb * LANES:(cb + 1) * LANES] = l_scr[cb]


def attn_band(q, k, v, g, dil):
    b, t, _ = q.shape
    span = ATTN_BLOCK * dil
    assert t % span == 0
    hq = Q_PER_KV * KV_PER_BRANCH * HEAD_DIM
    hk = KV_PER_BRANCH * HEAD_DIM
    cur = lambda bi, i: (bi, i, g)
    prev = lambda bi, i: (bi, jnp.maximum(i - 1, 0), g)
    osd = jax.ShapeDtypeStruct((b, t, hq), f32)
    nq = hq // LANES
    q_specs = [pl.BlockSpec((None, span, LANES), functools.partial(lambda bi, i, cb: (bi, i, g * nq + cb), cb=cb))
               for cb in range(nq)]
    return pl.pallas_call(
        functools.partial(_attn_band_kernel, dil=dil),
        out_shape=(osd, osd),
        grid=(b, t // span),
        in_specs=q_specs + [pl.BlockSpec((None, span, hk), cur), pl.BlockSpec((None, span, hk), prev),
                            pl.BlockSpec((None, span, hk), cur), pl.BlockSpec((None, span, hk), prev)],
        out_specs=(pl.BlockSpec((None, span, hq), lambda bi, i: (bi, i, 0)),
                   pl.BlockSpec((None, span, hq), lambda bi, i: (bi, i, 0))),
        scratch_shapes=[pltpu.VMEM((nq, span, LANES), f32), pltpu.VMEM((nq, span, LANES), f32)],
        compiler_params=_cparams(("parallel", "arbitrary")),
        name=f"attn_band_d{dil}",
    )(*([q] * nq), k, k, v, v)


def _merge3_kernel(o0, o1, o2, l0, l1, l2, out_ref):
    a0, a1, a2 = l0[...], l1[...], l2[...]
    m = jnp.maximum(jnp.maximum(a0, a1), a2)
    w0, w1, w2 = jnp.exp(a0 - m), jnp.exp(a1 - m), jnp.exp(a2 - m)
    out_ref[...] = ((w0 * o0[...] + w1 * o1[...] + w2 * o2[...]) / (w0 + w1 + w2)).astype(out_ref.dtype)


def merge3(os_, ls_, *, tm):
    n, c = os_[0].shape
    spec = pl.BlockSpec((tm, c), lambda i: (i, 0))
    return pl.pallas_call(
        _merge3_kernel,
        out_shape=jax.ShapeDtypeStruct((n, c), bf16),
        grid=(n // tm,),
        in_specs=[spec] * 6,
        out_specs=spec,
        compiler_params=_cparams(("parallel",)),
        name="attn_merge",
    )(*os_, *ls_)


def _attn_sample_kernel(q_ref, ck_ref, cv_ref, nk_ref, nv_ref, o_ref, *, t_new, win_buf):
    scale = HEAD_DIM ** -0.5
    q = q_ref[...] * scale
    n_cache = ck_ref.shape[0]
    n_new = nk_ref.shape[0]
    rows = Q_PER_KV * t_new
    t_c = lax.broadcasted_iota(jnp.int32, (rows, n_cache), 0) % t_new
    d_c = win_buf + t_c - lax.broadcasted_iota(jnp.int32, (rows, n_cache), 1)
    t_n = lax.broadcasted_iota(jnp.int32, (rows, n_new), 0) % t_new
    d_n = t_n - lax.broadcasted_iota(jnp.int32, (rows, n_new), 1)
    outs = [None] * (KV_PER_BRANCH * Q_PER_KV)
    for j in range(KV_PER_BRANCH):
        parts = []
        for g, (win, dil) in enumerate(DIL_PATTERNS):
            hq = (g * KV_PER_BRANCH + j) * Q_PER_KV
            hk = slice((g * KV_PER_BRANCH + j) * HEAD_DIM, (g * KV_PER_BRANCH + j + 1) * HEAD_DIM)
            qs = jnp.concatenate([q[:, (hq + r) * HEAD_DIM:(hq + r + 1) * HEAD_DIM] for r in range(Q_PER_KV)],
                                 axis=0).astype(bf16)
            ok_c = (d_c >= 0) & (d_c <= win) & ((d_c & (dil - 1)) == 0)
            ok_n = (d_n >= 0) & (d_n <= win) & ((d_n & (dil - 1)) == 0)
            s_c = jnp.where(ok_c, _dot_nt(qs, ck_ref[:, hk].astype(bf16)), NEG_INF)
            s_n = jnp.where(ok_n, _dot_nt(qs, nk_ref[:, hk].astype(bf16)), NEG_INF)
            parts.append((s_c, s_n, hk))
        m = None
        for s_c, s_n, _ in parts:
            mm = jnp.maximum(jnp.max(s_c, axis=-1, keepdims=True), jnp.max(s_n, axis=-1, keepdims=True))
            m = mm if m is None else jnp.maximum(m, mm)
        l = jnp.zeros_like(m)
        o = jnp.zeros((rows, HEAD_DIM), f32)
        for s_c, s_n, hk in parts:
            p_c = jnp.exp(s_c - m)
            p_n = jnp.exp(s_n - m)
            l = l + jnp.sum(p_c, axis=-1, keepdims=True) + jnp.sum(p_n, axis=-1, keepdims=True)
            o = o + jnp.dot(p_c.astype(bf16), cv_ref[:, hk].astype(bf16), preferred_element_type=f32)
            o = o + jnp.dot(p_n.astype(bf16), nv_ref[:, hk].astype(bf16), preferred_element_type=f32)
        o = o / l
        for r in range(Q_PER_KV):
            outs[j * Q_PER_KV + r] = o[r * t_new:(r + 1) * t_new]
    o_ref[...] = jnp.concatenate(outs, axis=1).astype(o_ref.dtype)


def attn_sample(q, ck, cv, nk, nv):
    b, t_new, qc = q.shape
    w = ck.shape[1]
    kc = ck.shape[2]
    n_new = nk.shape[1]
    return pl.pallas_call(
        functools.partial(_attn_sample_kernel, t_new=t_new, win_buf=w),
        out_shape=jax.ShapeDtypeStruct((b, t_new, ATTN_OUT_DIM), bf16),
        grid=(b,),
        in_specs=[pl.BlockSpec((None, t_new, qc), lambda i: (i, 0, 0)),
                  pl.BlockSpec((None, w, kc), lambda i: (i, 0, 0)),
                  pl.BlockSpec((None, w, kc), lambda i: (i, 0, 0)),
                  pl.BlockSpec((None, n_new, kc), lambda i: (i, 0, 0)),
                  pl.BlockSpec((None, n_new, kc), lambda i: (i, 0, 0))],
        out_specs=pl.BlockSpec((None, t_new, ATTN_OUT_DIM), lambda i: (i, 0, 0)),
        compiler_params=_cparams(("parallel",)),
        name="attn_sample",
    )(q, ck, cv, nk, nv)


def _prep_weights(norm_mix, norm_ffn, norm_ple, ssm_w_in, ssm_conv_w, ssm_conv_b, ssm_dt_bias, ssm_a_log, ssm_d,
                  ssm_norm, ssm_w_out, kv_norm, w_kv, k_norm, w_q, q_norm, w_o,
                  peer_w_query, peer_sub_keys, peer_u, peer_v, ple_w_proj, ple_w_gate):
    zx = D_INNER + CONV_DIM
    w_in = ssm_w_in[0]
    kdim = N_KV_HEADS * HEAD_DIM
    rep = LANES // HEAD_DIM
    return dict(
        norm_mix=norm_mix, norm_ffn=norm_ffn, norm_ple=norm_ple,
        w_z=w_in[:, :D_INNER].astype(bf16), w_xbc=w_in[:, D_INNER:zx].astype(bf16),
        w_dt=jnp.pad(w_in[:, zx:], ((0, 0), (0, LANES - SSM_HEADS))).astype(bf16),
        conv_w=ssm_conv_w[0], conv_b=ssm_conv_b[0], dt_bias=ssm_dt_bias[0], a_log=ssm_a_log[0],
        d_skip=ssm_d[0], ssm_norm=ssm_norm[0], w_out=ssm_w_out[0].astype(bf16),
        kv_norm=kv_norm, w_k=w_kv[:, :kdim].astype(bf16), w_v=w_kv[:, kdim:].astype(bf16),
        k_gain=jnp.tile(k_norm.astype(f32), rep).reshape(1, LANES),
        w_q=w_q[0].astype(bf16), q_gain=jnp.tile(q_norm[0].astype(f32), rep).reshape(1, LANES),
        w_o=w_o[0].astype(bf16),
        peer_wq=[peer_w_query[i].astype(bf16) for i in range(DEPTH)],
        peer_sk=[peer_sub_keys[i].astype(bf16) for i in range(DEPTH)],
        peer_u=[peer_u[i].astype(bf16) for i in range(DEPTH)],
        peer_vt=[peer_v[i].astype(bf16).T for i in range(DEPTH)],
        ple_wp=[ple_w_proj[i].astype(bf16) for i in range(DEPTH)],
        ple_wg=[ple_w_gate[i].astype(bf16) for i in range(DEPTH)],
    )


PEER_TOKEN_TILE = 512
PEER_SCORE_TILE = 256
PEER_EXPERT_TILE = 2 * SUBLANES * PEER_N_KEYS
PEER_SUB_TILE = 512


def _peer_ple(h, p, i, wts, tm):
    n = h.shape[0]
    tm_d = PEER_TOKEN_TILE if n % PEER_TOKEN_TILE == 0 else n
    xnt, rank2, n1, c1, e2 = peer_scores(h, wts["norm_ffn"][i], wts["peer_wq"][i], wts["peer_sk"][i],
                                         tm=PEER_SCORE_TILE)
    h = peer_dense(xnt, wts["peer_u"][i], wts["peer_vt"][i], rank2, n1, c1, e2, h,
                   tm=tm_d, te=PEER_EXPERT_TILE, sub=PEER_SUB_TILE)
    return fused_matmul(h, wts["ple_wg"][i], gain=wts["norm_ple"][i], mode="ple", res=h,
                        ple=(p, wts["ple_wp"][i]), tm=tm)


def _trunk(x, p, conv_state, ssm_state, past_k, past_v, pos0, wts):
    b, t, d = x.shape
    n = b * t
    prompt = past_k is None
    tm = _token_tile(n)
    h = x.reshape(n, d)

    z = fused_matmul(h, wts["w_z"], gain=wts["norm_mix"][0], tm=tm).reshape(b, t, D_INNER)
    xbc = fused_matmul(h, wts["w_xbc"], gain=wts["norm_mix"][0], tm=tm).reshape(b, t, CONV_DIM)
    dtr = fused_matmul(h, wts["w_dt"], gain=wts["norm_mix"][0], tm=tm)
    new_conv = xbc[:, t - (CONV_W - 1):, :] if t >= CONV_W - 1 else None
    dtr = dtr.reshape(b, t, LANES)
    tp = -(-t // SSD_CHUNK) * SSD_CHUNK
    if tp != t:
        padt = ((0, 0), (0, tp - t), (0, 0))
        z_p, xbc_p, dtr_p = jnp.pad(z, padt), jnp.pad(xbc, padt), jnp.pad(dtr, padt)
    else:
        z_p, xbc_p, dtr_p = z, xbc, dtr
    cb0 = jnp.pad(conv_state.astype(f32), ((0, 0), (SUBLANES - (CONV_W - 1), 0), (0, 0)))
    h0 = ssm_state.astype(f32).reshape(b, SSM_HEADS * SSM_HEAD_DIM, D_STATE)
    y, h_fin = ssd_mixer(z_p, xbc_p, dtr_p, h0, cb0, wts["conv_w"], wts["conv_b"], wts["dt_bias"], wts["a_log"],
                         wts["d_skip"], wts["ssm_norm"], t_valid=(None if tp == t else t))
    y = y[:, :t].reshape(n, D_INNER)
    h = fused_matmul(y, wts["w_out"], mode="add", res=h, tm=tm)
    new_ssm = h_fin.reshape(b, SSM_HEADS, SSM_HEAD_DIM, D_STATE)
    h = _peer_ple(h, p[0].reshape(n, -1), 0, wts, tm)

    pos = pos0 + jnp.arange(t, dtype=jnp.int32)
    tabs = _rope_tables(pos)
    if t < tm:
        tabs = tuple(jnp.tile(a, (tm // t, 1)) for a in tabs)
    k_new = fused_matmul(h, wts["w_k"], gain=wts["kv_norm"], mode="headrope",
                         headrope=(wts["k_gain"],) + tabs, tm=tm)
    v_new = fused_matmul(h, wts["w_v"], gain=wts["kv_norm"], tm=tm)

    q = fused_matmul(h, wts["w_q"], gain=wts["norm_mix"][1], mode="headrope",
                     headrope=(wts["q_gain"],) + tabs, tm=tm)
    kdim = N_KV_HEADS * HEAD_DIM
    if prompt:
        q3, k3, v3 = q.reshape(b, t, -1), k_new.reshape(b, t, kdim), v_new.reshape(b, t, kdim)
        os_, ls_ = [], []
        for g, (_, dil) in enumerate(DIL_PATTERNS):
            o_g, l_g = attn_band(q3, k3, v3, g, dil)
            os_.append(o_g.reshape(n, ATTN_OUT_DIM))
            ls_.append(l_g.reshape(n, ATTN_OUT_DIM))
        att = merge3(os_, ls_, tm=tm)
    else:
        padn = ((0, 0), (0, SSD_CHUNK - t), (0, 0))
        nk = jnp.pad(k_new.reshape(b, t, kdim), padn)
        nv = jnp.pad(v_new.reshape(b, t, kdim), padn)
        w = past_k.shape[1]
        att = attn_sample(q.reshape(b, t, -1), past_k.reshape(b, w, kdim).astype(f32),
                          past_v.reshape(b, w, kdim).astype(f32), nk, nv).reshape(n, ATTN_OUT_DIM)
    h = fused_matmul(att, wts["w_o"], mode="add", res=h, tm=tm)
    h = _peer_ple(h, p[1].reshape(n, -1), 1, wts, tm)

    return (h.reshape(b, t, d), new_conv[None], new_ssm[None],
            k_new.reshape(b, t, N_KV_HEADS, HEAD_DIM), v_new.reshape(b, t, N_KV_HEADS, HEAD_DIM))


def kernel(x_prompt, x_sample, state_conv, state_ssm, cache_k, cache_v, p_prompt, p_sample, norm_mix, norm_ffn, norm_ple, ssm_w_in, ssm_conv_w, ssm_conv_b, ssm_dt_bias, ssm_a_log, ssm_d, ssm_norm, ssm_w_out, kv_norm, w_kv, k_norm, w_q, q_norm, w_o, peer_w_query, peer_sub_keys, peer_u, peer_v, ple_w_proj, ple_w_gate):
    wts = _prep_weights(norm_mix, norm_ffn, norm_ple, ssm_w_in, ssm_conv_w, ssm_conv_b, ssm_dt_bias, ssm_a_log,
                        ssm_d, ssm_norm, ssm_w_out, kv_norm, w_kv, k_norm, w_q, q_norm, w_o,
                        peer_w_query, peer_sub_keys, peer_u, peer_v, ple_w_proj, ple_w_gate)
    b_p, t_p, _ = x_prompt.shape
    zero_conv = jnp.zeros((b_p, CONV_W - 1, CONV_DIM), x_prompt.dtype)
    zero_ssm = jnp.zeros((b_p, SSM_HEADS, SSM_HEAD_DIM, D_STATE), x_prompt.dtype)
    y_p, conv_p, ssm_p, k_p, v_p = _trunk(x_prompt, p_prompt, zero_conv, zero_ssm, None, None, 0, wts)
    keep = min(max(w for w, _ in DIL_PATTERNS), t_p)
    past_len = 16384
    y_s, conv_s, ssm_s, k_s, v_s = _trunk(x_sample, p_sample, state_conv[0], state_ssm[0], cache_k, cache_v,
                                          past_len, wts)
    if keep < t_p:
        k_p, v_p = k_p[:, -keep:], v_p[:, -keep:]
    return (y_p, y_s, conv_p, ssm_p, k_p, v_p, conv_s, ssm_s, k_s, v_s)
```

```python
import functools
import math

import jax
import jax.numpy as jnp
from jax import lax
from jax.experimental import pallas as pl
from jax.experimental.pallas import tpu as pltpu

f32 = jnp.float32
bf16 = jnp.bfloat16

D_MODEL = 1024
DEPTH = 2
N_A_LAYERS = 1
D_INNER = 2048
SSM_HEAD_DIM = 64
SSM_HEADS = 32
SSM_GROUPS = 8
SSM_HEADS_PER_GROUP = 4
D_STATE = 128
CONV_W = 4
CONV_DIM = 4096
SSD_CHUNK = 128
HEAD_DIM = 64
DIL_PATTERNS = ((128, 1), (512, 4), (2048, 16))
N_DIL = 3
KV_PER_BRANCH = 2
Q_PER_KV = 4
N_Q_HEADS = 24
N_KV_HEADS = 6
ATTN_OUT_DIM = 512
ROPE_DIM = 16
ROPE_THETA = 500000.0
PEER_HEADS = 8
PEER_N_KEYS = 128
PEER_TOPK = 16
PEER_HALF = 128
EPS = 1e-6

LANES = 128
SUBLANES = 8
VMEM_LIMIT_BYTES = 56 * 1024 * 1024

NEG_INF = float("-inf")


def _cparams(sem):
    return pltpu.CompilerParams(dimension_semantics=sem, vmem_limit_bytes=VMEM_LIMIT_BYTES)


def _dot_nt(a, b):
    return lax.dot_general(a, b, (((1,), (1,)), ((), ())), preferred_element_type=f32)


def _dot_tn(a, b):
    return lax.dot_general(a, b, (((0,), (0,)), ((), ())), preferred_element_type=f32)


def _sigmoid(x):
    return 1.0 / (1.0 + jnp.exp(-x))


def _silu(x):
    return x * _sigmoid(x)


def _softplus(x):
    return jnp.maximum(x, 0.0) + jnp.log1p(jnp.exp(-jnp.abs(x)))


def _gelu_tanh(x):
    c = math.sqrt(2.0 / math.pi)
    return x * (0.5 + 0.5 * jnp.tanh(x * (c + (c * 0.044715) * (x * x))))


def _fmm_kernel(*refs, has_norm, mode):
    it = iter(refs)
    x_ref = next(it)
    g_ref = next(it) if has_norm else None
    w_ref = next(it)
    res_ref = next(it) if mode in ("add", "ple") else None
    if mode == "ple":
        p_ref, wp_ref = next(it), next(it)
    if mode == "headrope":
        hg_ref, ta_ref, tb_ref, tc_ref, bd_ref = next(it), next(it), next(it), next(it), next(it)
    o_ref = next(it)
    xn_ref = next(it)

    @pl.when(pl.program_id(1) == 0)
    def _():
        x = x_ref[...].astype(f32)
        if has_norm:
            ms = jnp.mean(x * x, axis=-1, keepdims=True)
            x = x * lax.rsqrt(ms + EPS) * g_ref[...]
        xn_ref[...] = x.astype(bf16)

    acc = jnp.dot(xn_ref[...], w_ref[...], preferred_element_type=f32)
    if mode == "plain":
        o_ref[...] = acc.astype(o_ref.dtype)
    elif mode == "add":
        o_ref[...] = res_ref[...] + acc
    elif mode == "ple":
        pp = jnp.dot(p_ref[...].astype(bf16), wp_ref[...], preferred_element_type=f32)
        o_ref[...] = res_ref[...] + pp * _sigmoid(acc)
    elif mode == "headrope":
        tn = acc.shape[1]
        ta, tb, tc = ta_ref[...], tb_ref[...], tc_ref[...]
        hg = hg_ref[...]
        bd = bd_ref[...]
        for nb in range(tn // LANES):
            blk = acc[:, nb * LANES:(nb + 1) * LANES]
            sq = blk * blk
            sq_hi = sq.astype(bf16)
            sq_lo = (sq - sq_hi.astype(f32)).astype(bf16)
            ms = (jnp.dot(sq_hi, bd, preferred_element_type=f32) + jnp.dot(sq_lo, bd, preferred_element_type=f32))
            y = blk * lax.rsqrt(ms + EPS) * hg
            y = y * ta + pltpu.roll(y, LANES - ROPE_DIM // 2, 1) * tb + pltpu.roll(y, ROPE_DIM // 2, 1) * tc
            o_ref[:, nb * LANES:(nb + 1) * LANES] = y


MATMUL_VMEM_BUDGET = 40 * 1024 * 1024


def _token_tile(m):
    return next((c for c in (1024, 512, 256, 128) if m % c == 0), m)


def _col_tile(tm, k, n, x_bytes, mode, kp):
    out_blocks = 2 if mode in ("add", "ple") else 1
    for tn in sorted({n, 2048, 1536, 1024, 768, 512, 384, 256, 128}, reverse=True):
        if tn > n or n % tn:
            continue
        use = 2 * tm * k * x_bytes + tm * k * 2 + 2 * k * tn * 2 + 2 * out_blocks * tm * tn * 4
        if mode == "ple":
            use += 2 * tm * kp * 4 + 2 * kp * tn * 2
        if use <= MATMUL_VMEM_BUDGET:
            return tn
    raise ValueError("no column tile fits")


def fused_matmul(x, w, *, gain=None, mode="plain", res=None, ple=None, headrope=None, tm, tn=None, out_dtype=f32):
    m, k = x.shape
    n = w.shape[1]
    if tn is None:
        tn = _col_tile(tm, k, n, x.dtype.itemsize, mode, ple[0].shape[1] if ple else 0)
    assert m % tm == 0 and n % tn == 0, (m, tm, n, tn)
    has_norm = gain is not None
    args = [x]
    in_specs = [pl.BlockSpec((tm, k), lambda i, j: (i, 0))]
    if has_norm:
        args.append(gain.reshape(1, k).astype(f32))
        in_specs.append(pl.BlockSpec((1, k), lambda i, j: (0, 0)))
    args.append(w)
    in_specs.append(pl.BlockSpec((k, tn), lambda i, j: (0, j)))
    if mode in ("add", "ple"):
        args.append(res)
        in_specs.append(pl.BlockSpec((tm, tn), lambda i, j: (i, j)))
    if mode == "ple":
        p, wp = ple
        kp = p.shape[1]
        args += [p, wp]
        in_specs += [pl.BlockSpec((tm, kp), lambda i, j: (i, 0)), pl.BlockSpec((kp, tn), lambda i, j: (0, j))]
    if mode == "headrope":
        hg, ta, tb, tc = headrope
        period = ta.shape[0] // tm
        args += [hg, ta, tb, tc, _head_mean_matrix()]
        in_specs.append(pl.BlockSpec((1, LANES), lambda i, j: (0, 0)))
        for _ in range(3):
            in_specs.append(pl.BlockSpec((tm, LANES), lambda i, j: (i % period, 0)))
        in_specs.append(pl.BlockSpec((LANES, LANES), lambda i, j: (0, 0)))
    return pl.pallas_call(
        functools.partial(_fmm_kernel, has_norm=has_norm, mode=mode),
        out_shape=jax.ShapeDtypeStruct((m, n), out_dtype),
        grid=(m // tm, n // tn),
        in_specs=in_specs,
        out_specs=pl.BlockSpec((tm, tn), lambda i, j: (i, j)),
        scratch_shapes=[pltpu.VMEM((tm, k), bf16)],
        compiler_params=_cparams(("parallel", "arbitrary")),
        name="fused_matmul_" + mode,
    )(*args)


def _head_mean_matrix():
    r = jnp.arange(LANES) // HEAD_DIM
    return ((r[:, None] == r[None, :]).astype(f32) / HEAD_DIM).astype(bf16)


def _rope_tables(pos):
    half = ROPE_DIM // 2
    inv = ROPE_THETA ** (-jnp.arange(0, ROPE_DIM, 2, dtype=f32) / ROPE_DIM)
    ang = pos.astype(f32)[:, None] * inv[None, :]
    cos, sin = jnp.cos(ang), jnp.sin(ang)
    t = pos.shape[0]
    ones = jnp.ones((t, HEAD_DIM - ROPE_DIM), f32)
    zeros_h = jnp.zeros((t, half), f32)
    zeros_r = jnp.zeros((t, HEAD_DIM - ROPE_DIM), f32)
    ta = jnp.concatenate([cos, cos, ones], axis=1)
    tb = jnp.concatenate([-sin, zeros_h, zeros_r], axis=1)
    tc = jnp.concatenate([zeros_h, sin, zeros_r], axis=1)
    rep = LANES // HEAD_DIM
    return tuple(jnp.tile(a, (1, rep)) for a in (ta, tb, tc))


def _ssd_kernel(z_ref, xbc_ref, dt_ref, h0_ref, cb0_ref, cw_ref, cbias_ref, dtb_ref, alog_ref, dsk_ref, ng_ref,
                ex_ref, y_ref, hout_ref, h_scr, tail_scr, xc_scr, *, chunk, t_valid):
    c = pl.program_id(1)
    nc = pl.num_programs(1)
    L = chunk
    hp = SSM_HEADS_PER_GROUP * SSM_HEAD_DIM

    @pl.when(c == 0)
    def _():
        for g in range(SSM_GROUPS):
            h_scr[g] = h0_ref[g * hp:(g + 1) * hp, :].T
        tail_scr[0:SUBLANES, :] = cb0_ref[...]

    w = cw_ref[...]
    tail_scr[SUBLANES:2 * SUBLANES, :] = xbc_ref[0:SUBLANES, :]
    for lo, hi, src, off in ((0, SUBLANES, tail_scr, SUBLANES), (SUBLANES, L, xbc_ref, 0)):
        acc = src[lo + off:hi + off, :] * w[CONV_W - 1:CONV_W, :] + cbias_ref[...]
        for s in range(1, CONV_W):
            acc = acc + src[lo + off - s:hi + off - s, :] * w[CONV_W - 1 - s:CONV_W - s, :]
        xc_scr[lo:hi, :] = _silu(acc)
    tail_scr[0:SUBLANES, :] = xbc_ref[L - SUBLANES:L, :]

    dt = _softplus(dt_ref[...] + dtb_ref[...])
    if t_valid is not None:
        rowid = c * L + lax.broadcasted_iota(jnp.int32, (L, LANES), 0)
        dt = jnp.where(rowid < t_valid, dt, 0.0)
    a = -jnp.exp(alog_ref[...])
    da = dt * a
    r_i = lax.broadcasted_iota(jnp.int32, (L, L), 0)
    c_i = lax.broadcasted_iota(jnp.int32, (L, L), 1)
    causal = r_i >= c_i
    cum = jnp.dot(causal.astype(f32), da, precision=lax.Precision.HIGHEST, preferred_element_type=f32)
    cum_t = cum.T
    cum_last = cum[L - 1:L, :]
    tailw = jnp.exp(cum_last - cum) * dt
    ecum = jnp.exp(cum)

    src = jnp.concatenate([dt, tailw, ecum], axis=0)
    src_hi = src.astype(bf16)
    src_lo = (src - src_hi.astype(f32)).astype(bf16)
    pieces = jnp.concatenate([src_hi, src_lo], axis=0)

    for g in range(SSM_GROUPS):
        gs = slice(g * hp, (g + 1) * hp)
        ex = jnp.dot(pieces, ex_ref[:, gs], preferred_element_type=f32)
        dt_x = ex[0:L] + ex[3 * L:4 * L]
        tw_x = ex[L:2 * L] + ex[4 * L:5 * L]
        ec_x = ex[2 * L:3 * L] + ex[5 * L:6 * L]
        x_g = xc_scr[:, gs]
        b_g = xc_scr[:, D_INNER + g * D_STATE:D_INNER + (g + 1) * D_STATE].astype(bf16)
        c_g = xc_scr[:, D_INNER + SSM_GROUPS * D_STATE + g * D_STATE:
                     D_INNER + SSM_GROUPS * D_STATE + (g + 1) * D_STATE].astype(bf16)
        cb = _dot_nt(c_g, b_g)
        ht_g = h_scr[g]
        y_in = jnp.dot(c_g, ht_g.astype(bf16), preferred_element_type=f32)
        xdt = (x_g * dt_x).astype(bf16)
        ys = []
        for hh in range(SSM_HEADS_PER_GROUP):
            h = g * SSM_HEADS_PER_GROUP + hh
            seg = cum[:, h:h + 1] - cum_t[h:h + 1, :]
            dec = jnp.exp(jnp.where(causal, seg, NEG_INF))
            m = (cb * dec).astype(bf16)
            ys.append(jnp.dot(m, xdt[:, hh * SSM_HEAD_DIM:(hh + 1) * SSM_HEAD_DIM], preferred_element_type=f32))
        yg = jnp.concatenate(ys, axis=1) + y_in * ec_x + x_g * dsk_ref[:, gs]
        upd = _dot_tn(b_g, (x_g * tw_x).astype(bf16))
        h_scr[g] = ht_g * ec_x[L - 1:L, :] + upd
        yg = yg * _silu(z_ref[:, gs])
        yg = yg * lax.rsqrt(jnp.mean(yg * yg, axis=-1, keepdims=True) + EPS)
        y_ref[:, gs] = (yg * ng_ref[:, gs]).astype(y_ref.dtype)

    @pl.when(c == nc - 1)
    def _():
        for g in range(SSM_GROUPS):
            hout_ref[g * hp:(g + 1) * hp, :] = h_scr[g].T


def ssd_mixer(z, xbc, dtr, h0, cb0, conv_w, conv_b, dt_bias, a_log, d_skip, norm_g, *, t_valid):
    b, t, _ = z.shape
    L = SSD_CHUNK
    assert t % L == 0
    nc = t // L
    hp_all = SSM_HEADS * SSM_HEAD_DIM
    pad_h = LANES - SSM_HEADS

    def lane_pad(v):
        return jnp.pad(v.astype(f32), (0, pad_h)).reshape(1, LANES)

    cw = jnp.pad(conv_w.astype(f32), ((0, SUBLANES - CONV_W), (0, 0)))
    head_of_col = jnp.arange(hp_all) // SSM_HEAD_DIM
    expand = (jnp.arange(LANES)[:, None] == head_of_col[None, :]).astype(bf16)
    dsk_cols = jnp.repeat(d_skip.astype(f32), SSM_HEAD_DIM).reshape(1, hp_all)
    const = lambda shape: pl.BlockSpec(shape, lambda i, j: (0,) * len(shape))
    return pl.pallas_call(
        functools.partial(_ssd_kernel, chunk=L, t_valid=t_valid),
        out_shape=(jax.ShapeDtypeStruct((b, t, D_INNER), bf16),
                   jax.ShapeDtypeStruct((b, hp_all, D_STATE), f32)),
        grid=(b, nc),
        in_specs=[
            pl.BlockSpec((None, L, D_INNER), lambda i, j: (i, j, 0)),
            pl.BlockSpec((None, L, CONV_DIM), lambda i, j: (i, j, 0)),
            pl.BlockSpec((None, L, LANES), lambda i, j: (i, j, 0)),
            pl.BlockSpec((None, hp_all, D_STATE), lambda i, j: (i, 0, 0)),
            pl.BlockSpec((None, SUBLANES, CONV_DIM), lambda i, j: (i, 0, 0)),
            const((SUBLANES, CONV_DIM)),
            const((1, CONV_DIM)),
            const((1, LANES)), const((1, LANES)), const((1, D_INNER)),
            const((1, D_INNER)),
            const((LANES, hp_all)),
        ],
        out_specs=(pl.BlockSpec((None, L, D_INNER), lambda i, j: (i, j, 0)),
                   pl.BlockSpec((None, hp_all, D_STATE), lambda i, j: (i, 0, 0))),
        scratch_shapes=[pltpu.VMEM((SSM_GROUPS, D_STATE, hp_all // SSM_GROUPS), f32),
                        pltpu.VMEM((2 * SUBLANES, CONV_DIM), f32),
                        pltpu.VMEM((L, CONV_DIM), f32)],
        compiler_params=_cparams(("parallel", "arbitrary")),
        name="ssd_mixer",
    )(z, xbc, dtr, h0, cb0, cw, conv_b.reshape(1, CONV_DIM).astype(f32),
      lane_pad(dt_bias), lane_pad(a_log), dsk_cols, norm_g.reshape(1, D_INNER).astype(f32), expand)


def _sort_network(n):
    def merge(lo, hi, r):
        step = r * 2
        if step < hi - lo:
            yield from merge(lo, hi, step)
            yield from merge(lo + r, hi, step)
            yield from [(i, i + r) for i in range(lo + r, hi - r, step)]
        else:
            yield (lo, lo + r)

    def sort(lo, hi):
        if hi - lo >= 1:
            mid = lo + (hi - lo) // 2
            yield from sort(lo, mid)
            yield from sort(mid + 1, hi)
            yield from merge(lo, hi, 1)

    return list(sort(0, n - 1))


def _sublane_max(x):
    for shift in (4, 2, 1):
        x = jnp.maximum(x, pltpu.roll(x, shift, 0))
    return x


def _pop_columns(cols, extra, n):
    cols = list(cols)
    out = []
    for r in range(n):
        top = cols[0] if extra is None else jnp.maximum(cols[0], extra)
        m = _sublane_max(top)
        out.append(m)
        if r + 1 == n:
            break
        hit = cols[0] == m
        for i in range(min(n - r - 1, len(cols))):
            below = cols[i + 1] if i + 1 < len(cols) else NEG_INF
            cols[i] = jnp.where(hit, below, cols[i])
        if extra is not None:
            extra = jnp.where(extra == m, NEG_INF, extra)
    return out


def _top_values(s, n):
    k = s.shape[0] // SUBLANES
    v = [s[i * SUBLANES:(i + 1) * SUBLANES] for i in range(k)]
    for i, j in _sort_network(k):
        v[i], v[j] = jnp.maximum(v[i], v[j]), jnp.minimum(v[i], v[j])
    return _pop_columns(v[:n], None, n)


def _bf16_pair_words(x):
    hi = pltpu.bitcast(x.astype(bf16).astype(f32), jnp.uint32)
    return hi | (hi >> 16)


def _row_as_bf16_tile(words, rows, zero_words=None):
    tile = jnp.broadcast_to(words, (SUBLANES, words.shape[1]))
    if zero_words is not None:
        tile = tile + zero_words
    packed = pltpu.bitcast(tile, bf16)
    return jnp.concatenate([packed] * (rows // packed.shape[0]), axis=0)


def _peer_scores_kernel(h_ref, g_ref, wq_ref, sk_ref, xnt_ref, rank_ref, n1_ref, c1_ref, e2_ref, q_scr):
    x = h_ref[...]
    ms = jnp.mean(x * x, axis=-1, keepdims=True)
    xn = x * lax.rsqrt(ms + EPS) * g_ref[...]
    xnt_ref[...] = xn.T.astype(bf16)
    q = jnp.dot(xn.astype(bf16), wq_ref[...], preferred_element_type=f32).astype(bf16)
    for i in range(2 * PEER_HEADS):
        q_scr[i] = q[:, i * PEER_HALF:(i + 1) * PEER_HALF]
    kk = PEER_TOPK
    sk0 = sk_ref[0]
    sk1 = sk_ref[1]

    nkeys = sk0.shape[0]
    tokens = h_ref.shape[0]

    def rep(tile):
        return jnp.concatenate([tile] * (nkeys // SUBLANES), axis=0)

    def select_half(s1, s2):
        av = _top_values(s1, kk)
        bv = _top_values(s2, kk)
        sub = lax.broadcasted_iota(jnp.int32, av[0].shape, 0)
        a_lo, a_hi = av[SUBLANES - 1], av[2 * SUBLANES - 1]
        for r in range(SUBLANES - 2, -1, -1):
            a_lo = jnp.where(sub == r, av[r], a_lo)
            a_hi = jnp.where(sub == r, av[SUBLANES + r], a_hi)
        cand = [a_lo + bv[0]]
        for c in range(1, kk):
            cand.append(jnp.where(sub < kk // (c + 1), a_lo + bv[c], NEG_INF))
        best = _pop_columns(cand, a_hi + bv[0], kk)
        tau, top = best[kk - 1], best[0]
        zsum = jnp.zeros_like(top)
        for r in range(kk):
            zsum = zsum + jnp.exp(best[r] - top)
        tau_r = rep(tau)
        rank2 = jnp.full(s2.shape, float(kk), f32)
        n1 = jnp.zeros(s1.shape, f32)
        for c in range(kk - 1, -1, -1):
            b_c = rep(bv[c])
            rank2 = jnp.where(s2 >= b_c, float(c), rank2)
            n1 = n1 + jnp.where(s1 + b_c >= tau_r, 1.0, 0.0)
        c1 = jnp.exp(s1 - rep(av[0])) / rep(zsum)
        e2 = jnp.exp(s2 - rep(bv[0]))
        return rank2, n1, c1, e2

    def head(h, carry):
        s1 = _dot_nt(sk0, q_scr[2 * h])
        s2 = _dot_nt(sk1, q_scr[2 * h + 1])
        parts = [select_half(s1[:, l0:l0 + LANES], s2[:, l0:l0 + LANES]) for l0 in range(0, tokens, LANES)]
        rank2, n1, c1, e2 = (jnp.concatenate(p, axis=1) for p in zip(*parts))
        rank_ref[h] = rank2.astype(rank_ref.dtype)
        n1_ref[h] = _bf16_pair_words(n1)
        c1_ref[h] = _bf16_pair_words(c1)
        e2_ref[h] = e2.astype(e2_ref.dtype)
        return carry

    lax.fori_loop(0, PEER_HEADS, head, 0, unroll=True)


def peer_scores(h, gain, wq, sk, *, tm):
    n, d = h.shape
    nq = wq.shape[1]
    sel16 = jax.ShapeDtypeStruct((PEER_HEADS, PEER_N_KEYS, n), bf16)
    sel32 = jax.ShapeDtypeStruct((PEER_HEADS, PEER_N_KEYS, n), jnp.uint32)
    sel_spec = pl.BlockSpec((PEER_HEADS, PEER_N_KEYS, tm), lambda i: (0, 0, i))
    return pl.pallas_call(
        _peer_scores_kernel,
        out_shape=(jax.ShapeDtypeStruct((d, n), bf16), sel16, sel32, sel32, sel16),
        grid=(n // tm,),
        in_specs=[pl.BlockSpec((tm, d), lambda i: (i, 0)),
                  pl.BlockSpec((1, d), lambda i: (0, 0)),
                  pl.BlockSpec((d, nq), lambda i: (0, 0)),
                  pl.BlockSpec((2, PEER_N_KEYS, PEER_HALF), lambda i: (0, 0, 0))],
        out_specs=(pl.BlockSpec((d, tm), lambda i: (0, i)), sel_spec, sel_spec, sel_spec, sel_spec),
        scratch_shapes=[pltpu.VMEM((2 * PEER_HEADS, tm, PEER_HALF), bf16)],
        compiler_params=_cparams(("parallel",)),
        name="peer_scores",
    )(h, gain.reshape(1, d).astype(f32), wq, sk)


def _peer_dense_kernel(xnt_ref, u_ref, vt_ref, rank_ref, n1_ref, c1_ref, e2_ref, h_ref, o_ref,
                       acc_scr, coef_a, coef_b, *, te, sub, ne):
    j = pl.program_id(1)
    nk = PEER_N_KEYS
    a_per_step = te // nk
    a_per_sub = sub // nk
    tm = xnt_ref.shape[1]
    pk = 2 * SUBLANES
    zero = jnp.zeros((), bf16)

    @pl.when(j == 0)
    def _():
        acc_scr[...] = jnp.zeros_like(acc_scr)
        coef_b[...] = jnp.zeros_like(coef_b)

    def drain(src, part, parts):
        d = acc_scr.shape[0]
        rows = slice(part * d // parts, (part + 1) * d // parts)
        upd = jnp.dot(vt_ref[rows, :], src[...], preferred_element_type=f32)
        acc_scr[rows, :] += upd
        bits = pltpu.bitcast(upd[0:SUBLANES, :], jnp.uint32)
        return lax.shift_right_logical(bits, jnp.uint32(32))

    def build(dst, src):
        xnt = xnt_ref[...]
        nsub = te // sub

        def scores(s):
            return jnp.dot(u_ref[s * sub:(s + 1) * sub, :], xnt, preferred_element_type=f32)

        def masks(act, s, token):
            for ai in range(a_per_sub):
                ar = s * a_per_sub + ai
                grp = j * (a_per_step // SUBLANES) + ar // SUBLANES
                row = ar % SUBLANES
                wsum = None
                for h in range(PEER_HEADS):
                    tie = token if (h == 0 and token is not None) else None
                    n1a = _row_as_bf16_tile(n1_ref[h, grp, row:row + 1, :], nk, tie)
                    c1a = _row_as_bf16_tile(c1_ref[h, grp, row:row + 1, :], nk)
                    term = jnp.where(rank_ref[h] < n1a, e2_ref[h], zero) * c1a
                    wsum = term if wsum is None else wsum + term
                g = _gelu_tanh(act[ai * nk:(ai + 1) * nk].astype(bf16))
                dst[ar * nk:(ar + 1) * nk, :] = wsum * g

        token = None
        for s in range(nsub):
            act = scores(s)
            nxt = drain(src, s, nsub)
            masks(act, s, token)
            token = nxt

    @pl.when(jnp.logical_and(j < ne, j % 2 == 0))
    def _():
        build(coef_a, coef_b)

    @pl.when(jnp.logical_and(j < ne, j % 2 == 1))
    def _():
        build(coef_b, coef_a)

    @pl.when(j == ne)
    def _():
        drain(coef_b if ne % 2 == 0 else coef_a, 0, 1)
        o_ref[...] = h_ref[...] + acc_scr[...].T


def peer_dense(xnt, u, vt, rank2, n1, c1, e2, h, *, tm, te, sub):
    n, d = h.shape
    n_exp = u.shape[0]
    ne = n_exp // te
    assert (te // PEER_N_KEYS) % SUBLANES == 0
    sel_spec = pl.BlockSpec((PEER_HEADS, PEER_N_KEYS, tm), lambda i, j: (0, 0, i))
    ngrp = PEER_N_KEYS // SUBLANES
    n1 = n1.reshape(PEER_HEADS, ngrp, SUBLANES, n)
    c1 = c1.reshape(PEER_HEADS, ngrp, SUBLANES, n)
    row_spec = pl.BlockSpec((PEER_HEADS, ngrp, SUBLANES, tm), lambda i, j: (0, 0, 0, i))
    return pl.pallas_call(
        functools.partial(_peer_dense_kernel, te=te, sub=sub, ne=ne),
        out_shape=jax.ShapeDtypeStruct((n, d), f32),
        grid=(n // tm, ne + 1),
        in_specs=[pl.BlockSpec((d, tm), lambda i, j: (0, i)),
                  pl.BlockSpec((te, d), lambda i, j: (jnp.minimum(j, ne - 1), 0)),
                  pl.BlockSpec((d, te), lambda i, j: (0, jnp.maximum(j - 1, 0))),
                  sel_spec, row_spec, row_spec, sel_spec,
                  pl.BlockSpec((tm, d), lambda i, j: (i, 0))],
        out_specs=pl.BlockSpec((tm, d), lambda i, j: (i, 0)),
        scratch_shapes=[pltpu.VMEM((d, tm), f32), pltpu.VMEM((te, tm), bf16), pltpu.VMEM((te, tm), bf16)],
        compiler_params=_cparams(("parallel", "arbitrary")),
        name="peer_dense",
    )(xnt, u, vt, rank2, n1, c1, e2, h)


ATTN_BLOCK = 128


def _attn_band_kernel(*refs, dil):
    nq = Q_PER_KV * KV_PER_BRANCH * HEAD_DIM // LANES
    q_refs = refs[:nq]
    kc_ref, kp_ref, vc_ref, vp_ref, o_ref, l_ref, o_scr, l_scr = refs[nq:]
    i = pl.program_id(1)
    blk = ATTN_BLOCK
    scale = HEAD_DIM ** -0.5
    rows = Q_PER_KV * blk
    a_i = lax.broadcasted_iota(jnp.int32, (rows, blk), 0) % blk
    c_i = lax.broadcasted_iota(jnp.int32, (rows, blk), 1)
    mask_c = c_i <= a_i
    mask_p = jnp.logical_and(c_i >= a_i, i > 0)

    def residue(r, carry):
        sel = pl.ds(r, blk, stride=dil) if dil > 1 else pl.ds(0, blk)
        q = jnp.concatenate([qr[sel, :] for qr in q_refs], axis=1) * scale
        kc, kp, vc, vp = kc_ref[sel, :], kp_ref[sel, :], vc_ref[sel, :], vp_ref[sel, :]
        outs, lses = [], []
        for j in range(KV_PER_BRANCH):
            hs = slice(j * HEAD_DIM, (j + 1) * HEAD_DIM)
            qs = jnp.concatenate([q[:, (j * Q_PER_KV + x) * HEAD_DIM:(j * Q_PER_KV + x + 1) * HEAD_DIM]
                                  for x in range(Q_PER_KV)], axis=0).astype(bf16)
            s_c = jnp.where(mask_c, _dot_nt(qs, kc[:, hs].astype(bf16)), NEG_INF)
            s_p = jnp.where(mask_p, _dot_nt(qs, kp[:, hs].astype(bf16)), NEG_INF)
            m = jnp.maximum(jnp.max(s_c, axis=-1, keepdims=True), jnp.max(s_p, axis=-1, keepdims=True))
            p_c = jnp.exp(s_c - m)
            p_p = jnp.exp(s_p - m)
            l = jnp.sum(p_c, axis=-1, keepdims=True) + jnp.sum(p_p, axis=-1, keepdims=True)
            o = jnp.dot(p_c.astype(bf16), vc[:, hs].astype(bf16), preferred_element_type=f32)
            o = o + jnp.dot(p_p.astype(bf16), vp[:, hs].astype(bf16), preferred_element_type=f32)
            o = o / l
            lse = jnp.broadcast_to(m + jnp.log(l), (rows, HEAD_DIM))
            for x in range(Q_PER_KV):
                outs.append(o[x * blk:(x + 1) * blk])
                lses.append(lse[x * blk:(x + 1) * blk])
        per = LANES // HEAD_DIM
        for cb in range(nq):
            o_scr[cb, sel, :] = jnp.concatenate(outs[cb * per:(cb + 1) * per], axis=1)
            l_scr[cb, sel, :] = jnp.concatenate(lses[cb * per:(cb + 1) * per], axis=1)
        return carry

    lax.fori_loop(0, dil, residue, 0)
    for cb in range(nq):
        o_ref[:, cb * LANES:(cb + 1) * LANES] = o_scr[cb]
        l_ref[:, cb * LANES:(cb + 1) * LANES] = l_scr[cb]


def attn_band(q, k, v, g, dil):
    b, t, _ = q.shape
    span = ATTN_BLOCK * dil
    assert t % span == 0
    hq = Q_PER_KV * KV_PER_BRANCH * HEAD_DIM
    hk = KV_PER_BRANCH * HEAD_DIM
    cur = lambda bi, i: (bi, i, g)
    prev = lambda bi, i: (bi, jnp.maximum(i - 1, 0), g)
    osd = jax.ShapeDtypeStruct((b, t, hq), f32)
    nq = hq // LANES
    q_specs = [pl.BlockSpec((None, span, LANES), functools.partial(lambda bi, i, cb: (bi, i, g * nq + cb), cb=cb))
               for cb in range(nq)]
    return pl.pallas_call(
        functools.partial(_attn_band_kernel, dil=dil),
        out_shape=(osd, osd),
        grid=(b, t // span),
        in_specs=q_specs + [pl.BlockSpec((None, span, hk), cur), pl.BlockSpec((None, span, hk), prev),
                            pl.BlockSpec((None, span, hk), cur), pl.BlockSpec((None, span, hk), prev)],
        out_specs=(pl.BlockSpec((None, span, hq), lambda bi, i: (bi, i, 0)),
                   pl.BlockSpec((None, span, hq), lambda bi, i: (bi, i, 0))),
        scratch_shapes=[pltpu.VMEM((nq, span, LANES), f32), pltpu.VMEM((nq, span, LANES), f32)],
        compiler_params=_cparams(("parallel", "arbitrary")),
        name=f"attn_band_d{dil}",
    )(*([q] * nq), k, k, v, v)


def _merge3_kernel(o0, o1, o2, l0, l1, l2, out_ref):
    a0, a1, a2 = l0[...], l1[...], l2[...]
    m = jnp.maximum(jnp.maximum(a0, a1), a2)
    w0, w1, w2 = jnp.exp(a0 - m), jnp.exp(a1 - m), jnp.exp(a2 - m)
    out_ref[...] = ((w0 * o0[...] + w1 * o1[...] + w2 * o2[...]) / (w0 + w1 + w2)).astype(out_ref.dtype)


def merge3(os_, ls_, *, tm):
    n, c = os_[0].shape
    spec = pl.BlockSpec((tm, c), lambda i: (i, 0))
    return pl.pallas_call(
        _merge3_kernel,
        out_shape=jax.ShapeDtypeStruct((n, c), bf16),
        grid=(n // tm,),
        in_specs=[spec] * 6,
        out_specs=spec,
        compiler_params=_cparams(("parallel",)),
        name="attn_merge",
    )(*os_, *ls_)


def _attn_sample_kernel(q_ref, ck_ref, cv_ref, nk_ref, nv_ref, o_ref, *, t_new, win_buf):
    scale = HEAD_DIM ** -0.5
    q = q_ref[...] * scale
    n_cache = ck_ref.shape[0]
    n_new = nk_ref.shape[0]
    rows = Q_PER_KV * t_new
    t_c = lax.broadcasted_iota(jnp.int32, (rows, n_cache), 0) % t_new
    d_c = win_buf + t_c - lax.broadcasted_iota(jnp.int32, (rows, n_cache), 1)
    t_n = lax.broadcasted_iota(jnp.int32, (rows, n_new), 0) % t_new
    d_n = t_n - lax.broadcasted_iota(jnp.int32, (rows, n_new), 1)
    outs = [None] * (KV_PER_BRANCH * Q_PER_KV)
    for j in range(KV_PER_BRANCH):
        parts = []
        for g, (win, dil) in enumerate(DIL_PATTERNS):
            hq = (g * KV_PER_BRANCH + j) * Q_PER_KV
            hk = slice((g * KV_PER_BRANCH + j) * HEAD_DIM, (g * KV_PER_BRANCH + j + 1) * HEAD_DIM)
            qs = jnp.concatenate([q[:, (hq + r) * HEAD_DIM:(hq + r + 1) * HEAD_DIM] for r in range(Q_PER_KV)],
                                 axis=0).astype(bf16)
            ok_c = (d_c >= 0) & (d_c <= win) & ((d_c & (dil - 1)) == 0)
            ok_n = (d_n >= 0) & (d_n <= win) & ((d_n & (dil - 1)) == 0)
            s_c = jnp.where(ok_c, _dot_nt(qs, ck_ref[:, hk].astype(bf16)), NEG_INF)
            s_n = jnp.where(ok_n, _dot_nt(qs, nk_ref[:, hk].astype(bf16)), NEG_INF)
            parts.append((s_c, s_n, hk))
        m = None
        for s_c, s_n, _ in parts:
            mm = jnp.maximum(jnp.max(s_c, axis=-1, keepdims=True), jnp.max(s_n, axis=-1, keepdims=True))
            m = mm if m is None else jnp.maximum(m, mm)
        l = jnp.zeros_like(m)
        o = jnp.zeros((rows, HEAD_DIM), f32)
        for s_c, s_n, hk in parts:
            p_c = jnp.exp(s_c - m)
            p_n = jnp.exp(s_n - m)
            l = l + jnp.sum(p_c, axis=-1, keepdims=True) + jnp.sum(p_n, axis=-1, keepdims=True)
            o = o + jnp.dot(p_c.astype(bf16), cv_ref[:, hk].astype(bf16), preferred_element_type=f32)
            o = o + jnp.dot(p_n.astype(bf16), nv_ref[:, hk].astype(bf16), preferred_element_type=f32)
        o = o / l
        for r in range(Q_PER_KV):
            outs[j * Q_PER_KV + r] = o[r * t_new:(r + 1) * t_new]
    o_ref[...] = jnp.concatenate(outs, axis=1).astype(o_ref.dtype)


def attn_sample(q, ck, cv, nk, nv):
    b, t_new, qc = q.shape
    w = ck.shape[1]
    kc = ck.shape[2]
    n_new = nk.shape[1]
    return pl.pallas_call(
        functools.partial(_attn_sample_kernel, t_new=t_new, win_buf=w),
        out_shape=jax.ShapeDtypeStruct((b, t_new, ATTN_OUT_DIM), bf16),
        grid=(b,),
        in_specs=[pl.BlockSpec((None, t_new, qc), lambda i: (i, 0, 0)),
                  pl.BlockSpec((None, w, kc), lambda i: (i, 0, 0)),
                  pl.BlockSpec((None, w, kc), lambda i: (i, 0, 0)),
                  pl.BlockSpec((None, n_new, kc), lambda i: (i, 0, 0)),
                  pl.BlockSpec((None, n_new, kc), lambda i: (i, 0, 0))],
        out_specs=pl.BlockSpec((None, t_new, ATTN_OUT_DIM), lambda i: (i, 0, 0)),
        compiler_params=_cparams(("parallel",)),
        name="attn_sample",
    )(q, ck, cv, nk, nv)


def _prep_weights(norm_mix, norm_ffn, norm_ple, ssm_w_in, ssm_conv_w, ssm_conv_b, ssm_dt_bias, ssm_a_log, ssm_d,
                  ssm_norm, ssm_w_out, kv_norm, w_kv, k_norm, w_q, q_norm, w_o,
                  peer_w_query, peer_sub_keys, peer_u, peer_v, ple_w_proj, ple_w_gate):
    zx = D_INNER + CONV_DIM
    w_in = ssm_w_in[0]
    kdim = N_KV_HEADS * HEAD_DIM
    rep = LANES // HEAD_DIM
    return dict(
        norm_mix=norm_mix, norm_ffn=norm_ffn, norm_ple=norm_ple,
        w_z=w_in[:, :D_INNER].astype(bf16), w_xbc=w_in[:, D_INNER:zx].astype(bf16),
        w_dt=jnp.pad(w_in[:, zx:], ((0, 0), (0, LANES - SSM_HEADS))).astype(bf16),
        conv_w=ssm_conv_w[0], conv_b=ssm_conv_b[0], dt_bias=ssm_dt_bias[0], a_log=ssm_a_log[0],
        d_skip=ssm_d[0], ssm_norm=ssm_norm[0], w_out=ssm_w_out[0].astype(bf16),
        kv_norm=kv_norm, w_k=w_kv[:, :kdim].astype(bf16), w_v=w_kv[:, kdim:].astype(bf16),
        k_gain=jnp.tile(k_norm.astype(f32), rep).reshape(1, LANES),
        w_q=w_q[0].astype(bf16), q_gain=jnp.tile(q_norm[0].astype(f32), rep).reshape(1, LANES),
        w_o=w_o[0].astype(bf16),
        peer_wq=[peer_w_query[i].astype(bf16) for i in range(DEPTH)],
        peer_sk=[peer_sub_keys[i].astype(bf16) for i in range(DEPTH)],
        peer_u=[peer_u[i].astype(bf16) for i in range(DEPTH)],
        peer_vt=[peer_v[i].astype(bf16).T for i in range(DEPTH)],
        ple_wp=[ple_w_proj[i].astype(bf16) for i in range(DEPTH)],
        ple_wg=[ple_w_gate[i].astype(bf16) for i in range(DEPTH)],
    )


PEER_TOKEN_TILE = 512
PEER_SCORE_TILE = 256
PEER_EXPERT_TILE = 2 * SUBLANES * PEER_N_KEYS
PEER_SUB_TILE = 1024


def _peer_ple(h, p, i, wts, tm):
    n = h.shape[0]
    tm_d = PEER_TOKEN_TILE if n % PEER_TOKEN_TILE == 0 else n
    xnt, rank2, n1, c1, e2 = peer_scores(h, wts["norm_ffn"][i], wts["peer_wq"][i], wts["peer_sk"][i],
                                         tm=PEER_SCORE_TILE)
    h = peer_dense(xnt, wts["peer_u"][i], wts["peer_vt"][i], rank2, n1, c1, e2, h,
                   tm=tm_d, te=PEER_EXPERT_TILE, sub=PEER_SUB_TILE)
    return fused_matmul(h, wts["ple_wg"][i], gain=wts["norm_ple"][i], mode="ple", res=h,
                        ple=(p, wts["ple_wp"][i]), tm=tm)


def _trunk(x, p, conv_state, ssm_state, past_k, past_v, pos0, wts):
    b, t, d = x.shape
    n = b * t
    prompt = past_k is None
    tm = _token_tile(n)
    h = x.reshape(n, d)

    z = fused_matmul(h, wts["w_z"], gain=wts["norm_mix"][0], tm=tm).reshape(b, t, D_INNER)
    xbc = fused_matmul(h, wts["w_xbc"], gain=wts["norm_mix"][0], tm=tm).reshape(b, t, CONV_DIM)
    dtr = fused_matmul(h, wts["w_dt"], gain=wts["norm_mix"][0], tm=tm)
    new_conv = xbc[:, t - (CONV_W - 1):, :] if t >= CONV_W - 1 else None
    dtr = dtr.reshape(b, t, LANES)
    tp = -(-t // SSD_CHUNK) * SSD_CHUNK
    if tp != t:
        padt = ((0, 0), (0, tp - t), (0, 0))
        z_p, xbc_p, dtr_p = jnp.pad(z, padt), jnp.pad(xbc, padt), jnp.pad(dtr, padt)
    else:
        z_p, xbc_p, dtr_p = z, xbc, dtr
    cb0 = jnp.pad(conv_state.astype(f32), ((0, 0), (SUBLANES - (CONV_W - 1), 0), (0, 0)))
    h0 = ssm_state.astype(f32).reshape(b, SSM_HEADS * SSM_HEAD_DIM, D_STATE)
    y, h_fin = ssd_mixer(z_p, xbc_p, dtr_p, h0, cb0, wts["conv_w"], wts["conv_b"], wts["dt_bias"], wts["a_log"],
                         wts["d_skip"], wts["ssm_norm"], t_valid=(None if tp == t else t))
    y = y[:, :t].reshape(n, D_INNER)
    h = fused_matmul(y, wts["w_out"], mode="add", res=h, tm=tm)
    new_ssm = h_fin.reshape(b, SSM_HEADS, SSM_HEAD_DIM, D_STATE)
    h = _peer_ple(h, p[0].reshape(n, -1), 0, wts, tm)

    pos = pos0 + jnp.arange(t, dtype=jnp.int32)
    tabs = _rope_tables(pos)
    if t < tm:
        tabs = tuple(jnp.tile(a, (tm // t, 1)) for a in tabs)
    k_new = fused_matmul(h, wts["w_k"], gain=wts["kv_norm"], mode="headrope",
                         headrope=(wts["k_gain"],) + tabs, tm=tm)
    v_new = fused_matmul(h, wts["w_v"], gain=wts["kv_norm"], tm=tm)

    q = fused_matmul(h, wts["w_q"], gain=wts["norm_mix"][1], mode="headrope",
                     headrope=(wts["q_gain"],) + tabs, tm=tm)
    kdim = N_KV_HEADS * HEAD_DIM
    if prompt:
        q3, k3, v3 = q.reshape(b, t, -1), k_new.reshape(b, t, kdim), v_new.reshape(b, t, kdim)
        os_, ls_ = [], []
        for g, (_, dil) in enumerate(DIL_PATTERNS):
            o_g, l_g = attn_band(q3, k3, v3, g, dil)
            os_.append(o_g.reshape(n, ATTN_OUT_DIM))
            ls_.append(l_g.reshape(n, ATTN_OUT_DIM))
        att = merge3(os_, ls_, tm=tm)
    else:
        padn = ((0, 0), (0, SSD_CHUNK - t), (0, 0))
        nk = jnp.pad(k_new.reshape(b, t, kdim), padn)
        nv = jnp.pad(v_new.reshape(b, t, kdim), padn)
        w = past_k.shape[1]
        att = attn_sample(q.reshape(b, t, -1), past_k.reshape(b, w, kdim).astype(f32),
                          past_v.reshape(b, w, kdim).astype(f32), nk, nv).reshape(n, ATTN_OUT_DIM)
    h = fused_matmul(att, wts["w_o"], mode="add", res=h, tm=tm)
    h = _peer_ple(h, p[1].reshape(n, -1), 1, wts, tm)

    return (h.reshape(b, t, d), new_conv[None], new_ssm[None],
            k_new.reshape(b, t, N_KV_HEADS, HEAD_DIM), v_new.reshape(b, t, N_KV_HEADS, HEAD_DIM))


def kernel(x_prompt, x_sample, state_conv, state_ssm, cache_k, cache_v, p_prompt, p_sample, norm_mix, norm_ffn, norm_ple, ssm_w_in, ssm_conv_w, ssm_conv_b, ssm_dt_bias, ssm_a_log, ssm_d, ssm_norm, ssm_w_out, kv_norm, w_kv, k_norm, w_q, q_norm, w_o, peer_w_query, peer_sub_keys, peer_u, peer_v, ple_w_proj, ple_w_gate):
    wts = _prep_weights(norm_mix, norm_ffn, norm_ple, ssm_w_in, ssm_conv_w, ssm_conv_b, ssm_dt_bias, ssm_a_log,
                        ssm_d, ssm_norm, ssm_w_out, kv_norm, w_kv, k_norm, w_q, q_norm, w_o,
                        peer_w_query, peer_sub_keys, peer_u, peer_v, ple_w_proj, ple_w_gate)
    b_p, t_p, _ = x_prompt.shape
    zero_conv = jnp.zeros((b_p, CONV_W - 1, CONV_DIM), x_prompt.dtype)
    zero_ssm = jnp.zeros((b_p, SSM_HEADS, SSM_HEAD_DIM, D_STATE), x_prompt.dtype)
    y_p, conv_p, ssm_p, k_p, v_p = _trunk(x_prompt, p_prompt, zero_conv, zero_ssm, None, None, 0, wts)
    keep = min(max(w for w, _ in DIL_PATTERNS), t_p)
    past_len = 16384
    y_s, conv_s, ssm_s, k_s, v_s = _trunk(x_sample, p_sample, state_conv[0], state_ssm[0], cache_k, cache_v,
                                          past_len, wts)
    if keep < t_p:
        k_p, v_p = k_p[:, -keep:], v_p[:, -keep:]
    return (y_p, y_s, conv_p, ssm_p, k_p, v_p, conv_s, ssm_s, k_s, v_s)
```

```python
import functools
import math

import jax
import jax.numpy as jnp
from jax import lax
from jax.experimental import pallas as pl
from jax.experimental.pallas import tpu as pltpu

f32 = jnp.float32
bf16 = jnp.bfloat16

D_MODEL = 1024
DEPTH = 2
N_A_LAYERS = 1
D_INNER = 2048
SSM_HEAD_DIM = 64
SSM_HEADS = 32
SSM_GROUPS = 8
SSM_HEADS_PER_GROUP = 4
D_STATE = 128
CONV_W = 4
CONV_DIM = 4096
SSD_CHUNK = 128
HEAD_DIM = 64
DIL_PATTERNS = ((128, 1), (512, 4), (2048, 16))
N_DIL = 3
KV_PER_BRANCH = 2
Q_PER_KV = 4
N_Q_HEADS = 24
N_KV_HEADS = 6
ATTN_OUT_DIM = 512
ROPE_DIM = 16
ROPE_THETA = 500000.0
PEER_HEADS = 8
PEER_N_KEYS = 128
PEER_TOPK = 16
PEER_HALF = 128
EPS = 1e-6

LANES = 128
SUBLANES = 8
VMEM_LIMIT_BYTES = 56 * 1024 * 1024

NEG_INF = float("-inf")


def _cparams(sem):
    return pltpu.CompilerParams(dimension_semantics=sem, vmem_limit_bytes=VMEM_LIMIT_BYTES)


def _dot_nt(a, b):
    return lax.dot_general(a, b, (((1,), (1,)), ((), ())), preferred_element_type=f32)


def _dot_tn(a, b):
    return lax.dot_general(a, b, (((0,), (0,)), ((), ())), preferred_element_type=f32)


def _sigmoid(x):
    return 1.0 / (1.0 + jnp.exp(-x))


def _silu(x):
    return x * _sigmoid(x)


def _softplus(x):
    return jnp.maximum(x, 0.0) + jnp.log1p(jnp.exp(-jnp.abs(x)))


def _gelu_tanh(x):
    c = math.sqrt(2.0 / math.pi)
    return x * (0.5 + 0.5 * jnp.tanh(x * (c + (c * 0.044715) * (x * x))))


def _fmm_kernel(*refs, has_norm, mode):
    it = iter(refs)
    x_ref = next(it)
    g_ref = next(it) if has_norm else None
    w_ref = next(it)
    res_ref = next(it) if mode in ("add", "ple") else None
    if mode == "ple":
        p_ref, wp_ref = next(it), next(it)
    if mode == "headrope":
        hg_ref, ta_ref, tb_ref, tc_ref, bd_ref = next(it), next(it), next(it), next(it), next(it)
    o_ref = next(it)
    xn_ref = next(it)

    @pl.when(pl.program_id(1) == 0)
    def _():
        x = x_ref[...].astype(f32)
        if has_norm:
            ms = jnp.mean(x * x, axis=-1, keepdims=True)
            x = x * lax.rsqrt(ms + EPS) * g_ref[...]
        xn_ref[...] = x.astype(bf16)

    acc = jnp.dot(xn_ref[...], w_ref[...], preferred_element_type=f32)
    if mode == "plain":
        o_ref[...] = acc.astype(o_ref.dtype)
    elif mode == "add":
        o_ref[...] = res_ref[...] + acc
    elif mode == "ple":
        pp = jnp.dot(p_ref[...].astype(bf16), wp_ref[...], preferred_element_type=f32)
        o_ref[...] = res_ref[...] + pp * _sigmoid(acc)
    elif mode == "headrope":
        tn = acc.shape[1]
        ta, tb, tc = ta_ref[...], tb_ref[...], tc_ref[...]
        hg = hg_ref[...]
        bd = bd_ref[...]
        for nb in range(tn // LANES):
            blk = acc[:, nb * LANES:(nb + 1) * LANES]
            sq = blk * blk
            sq_hi = sq.astype(bf16)
            sq_lo = (sq - sq_hi.astype(f32)).astype(bf16)
            ms = (jnp.dot(sq_hi, bd, preferred_element_type=f32) + jnp.dot(sq_lo, bd, preferred_element_type=f32))
            y = blk * lax.rsqrt(ms + EPS) * hg
            y = y * ta + pltpu.roll(y, LANES - ROPE_DIM // 2, 1) * tb + pltpu.roll(y, ROPE_DIM // 2, 1) * tc
            o_ref[:, nb * LANES:(nb + 1) * LANES] = y


MATMUL_VMEM_BUDGET = 40 * 1024 * 1024


def _token_tile(m):
    return next((c for c in (1024, 512, 256, 128) if m % c == 0), m)


def _col_tile(tm, k, n, x_bytes, mode, kp):
    out_blocks = 2 if mode in ("add", "ple") else 1
    for tn in sorted({n, 2048, 1536, 1024, 768, 512, 384, 256, 128}, reverse=True):
        if tn > n or n % tn:
            continue
        use = 2 * tm * k * x_bytes + tm * k * 2 + 2 * k * tn * 2 + 2 * out_blocks * tm * tn * 4
        if mode == "ple":
            use += 2 * tm * kp * 4 + 2 * kp * tn * 2
        if use <= MATMUL_VMEM_BUDGET:
            return tn
    raise ValueError("no column tile fits")


def fused_matmul(x, w, *, gain=None, mode="plain", res=None, ple=None, headrope=None, tm, tn=None, out_dtype=f32):
    m, k = x.shape
    n = w.shape[1]
    if tn is None:
        tn = _col_tile(tm, k, n, x.dtype.itemsize, mode, ple[0].shape[1] if ple else 0)
    assert m % tm == 0 and n % tn == 0, (m, tm, n, tn)
    has_norm = gain is not None
    args = [x]
    in_specs = [pl.BlockSpec((tm, k), lambda i, j: (i, 0))]
    if has_norm:
        args.append(gain.reshape(1, k).astype(f32))
        in_specs.append(pl.BlockSpec((1, k), lambda i, j: (0, 0)))
    args.append(w)
    in_specs.append(pl.BlockSpec((k, tn), lambda i, j: (0, j)))
    if mode in ("add", "ple"):
        args.append(res)
        in_specs.append(pl.BlockSpec((tm, tn), lambda i, j: (i, j)))
    if mode == "ple":
        p, wp = ple
        kp = p.shape[1]
        args += [p, wp]
        in_specs += [pl.BlockSpec((tm, kp), lambda i, j: (i, 0)), pl.BlockSpec((kp, tn), lambda i, j: (0, j))]
    if mode == "headrope":
        hg, ta, tb, tc = headrope
        period = ta.shape[0] // tm
        args += [hg, ta, tb, tc, _head_mean_matrix()]
        in_specs.append(pl.BlockSpec((1, LANES), lambda i, j: (0, 0)))
        for _ in range(3):
            in_specs.append(pl.BlockSpec((tm, LANES), lambda i, j: (i % period, 0)))
        in_specs.append(pl.BlockSpec((LANES, LANES), lambda i, j: (0, 0)))
    return pl.pallas_call(
        functools.partial(_fmm_kernel, has_norm=has_norm, mode=mode),
        out_shape=jax.ShapeDtypeStruct((m, n), out_dtype),
        grid=(m // tm, n // tn),
        in_specs=in_specs,
        out_specs=pl.BlockSpec((tm, tn), lambda i, j: (i, j)),
        scratch_shapes=[pltpu.VMEM((tm, k), bf16)],
        compiler_params=_cparams(("parallel", "arbitrary")),
        name="fused_matmul_" + mode,
    )(*args)


def _head_mean_matrix():
    r = jnp.arange(LANES) // HEAD_DIM
    return ((r[:, None] == r[None, :]).astype(f32) / HEAD_DIM).astype(bf16)


def _rope_tables(pos):
    half = ROPE_DIM // 2
    inv = ROPE_THETA ** (-jnp.arange(0, ROPE_DIM, 2, dtype=f32) / ROPE_DIM)
    ang = pos.astype(f32)[:, None] * inv[None, :]
    cos, sin = jnp.cos(ang), jnp.sin(ang)
    t = pos.shape[0]
    ones = jnp.ones((t, HEAD_DIM - ROPE_DIM), f32)
    zeros_h = jnp.zeros((t, half), f32)
    zeros_r = jnp.zeros((t, HEAD_DIM - ROPE_DIM), f32)
    ta = jnp.concatenate([cos, cos, ones], axis=1)
    tb = jnp.concatenate([-sin, zeros_h, zeros_r], axis=1)
    tc = jnp.concatenate([zeros_h, sin, zeros_r], axis=1)
    rep = LANES // HEAD_DIM
    return tuple(jnp.tile(a, (1, rep)) for a in (ta, tb, tc))


def _ssd_kernel(z_ref, xbc_ref, dt_ref, h0_ref, cb0_ref, cw_ref, cbias_ref, dtb_ref, alog_ref, dsk_ref, ng_ref,
                ex_ref, y_ref, hout_ref, h_scr, tail_scr, xc_scr, *, chunk, t_valid):
    c = pl.program_id(1)
    nc = pl.num_programs(1)
    L = chunk
    hp = SSM_HEADS_PER_GROUP * SSM_HEAD_DIM

    @pl.when(c == 0)
    def _():
        for g in range(SSM_GROUPS):
            h_scr[g] = h0_ref[g * hp:(g + 1) * hp, :].T
        tail_scr[0:SUBLANES, :] = cb0_ref[...]

    w = cw_ref[...]
    tail_scr[SUBLANES:2 * SUBLANES, :] = xbc_ref[0:SUBLANES, :]
    for lo, hi, src, off in ((0, SUBLANES, tail_scr, SUBLANES), (SUBLANES, L, xbc_ref, 0)):
        acc = src[lo + off:hi + off, :] * w[CONV_W - 1:CONV_W, :] + cbias_ref[...]
        for s in range(1, CONV_W):
            acc = acc + src[lo + off - s:hi + off - s, :] * w[CONV_W - 1 - s:CONV_W - s, :]
        xc_scr[lo:hi, :] = _silu(acc)
    tail_scr[0:SUBLANES, :] = xbc_ref[L - SUBLANES:L, :]

    dt = _softplus(dt_ref[...] + dtb_ref[...])
    if t_valid is not None:
        rowid = c * L + lax.broadcasted_iota(jnp.int32, (L, LANES), 0)
        dt = jnp.where(rowid < t_valid, dt, 0.0)
    a = -jnp.exp(alog_ref[...])
    da = dt * a
    r_i = lax.broadcasted_iota(jnp.int32, (L, L), 0)
    c_i = lax.broadcasted_iota(jnp.int32, (L, L), 1)
    causal = r_i >= c_i
    cum = jnp.dot(causal.astype(f32), da, precision=lax.Precision.HIGHEST, preferred_element_type=f32)
    cum_t = cum.T
    cum_last = cum[L - 1:L, :]
    tailw = jnp.exp(cum_last - cum) * dt
    ecum = jnp.exp(cum)

    src = jnp.concatenate([dt, tailw, ecum], axis=0)
    src_hi = src.astype(bf16)
    src_lo = (src - src_hi.astype(f32)).astype(bf16)
    pieces = jnp.concatenate([src_hi, src_lo], axis=0)

    for g in range(SSM_GROUPS):
        gs = slice(g * hp, (g + 1) * hp)
        ex = jnp.dot(pieces, ex_ref[:, gs], preferred_element_type=f32)
        dt_x = ex[0:L] + ex[3 * L:4 * L]
        tw_x = ex[L:2 * L] + ex[4 * L:5 * L]
        ec_x = ex[2 * L:3 * L] + ex[5 * L:6 * L]
        x_g = xc_scr[:, gs]
        b_g = xc_scr[:, D_INNER + g * D_STATE:D_INNER + (g + 1) * D_STATE].astype(bf16)
        c_g = xc_scr[:, D_INNER + SSM_GROUPS * D_STATE + g * D_STATE:
                     D_INNER + SSM_GROUPS * D_STATE + (g + 1) * D_STATE].astype(bf16)
        cb = _dot_nt(c_g, b_g)
        ht_g = h_scr[g]
        y_in = jnp.dot(c_g, ht_g.astype(bf16), preferred_element_type=f32)
        xdt = (x_g * dt_x).astype(bf16)
        ys = []
        for hh in range(SSM_HEADS_PER_GROUP):
            h = g * SSM_HEADS_PER_GROUP + hh
            seg = cum[:, h:h + 1] - cum_t[h:h + 1, :]
            dec = jnp.exp(jnp.where(causal, seg, NEG_INF))
            m = (cb * dec).astype(bf16)
            ys.append(jnp.dot(m, xdt[:, hh * SSM_HEAD_DIM:(hh + 1) * SSM_HEAD_DIM], preferred_element_type=f32))
        yg = jnp.concatenate(ys, axis=1) + y_in * ec_x + x_g * dsk_ref[:, gs]
        upd = _dot_tn(b_g, (x_g * tw_x).astype(bf16))
        h_scr[g] = ht_g * ec_x[L - 1:L, :] + upd
        yg = yg * _silu(z_ref[:, gs])
        yg = yg * lax.rsqrt(jnp.mean(yg * yg, axis=-1, keepdims=True) + EPS)
        y_ref[:, gs] = (yg * ng_ref[:, gs]).astype(y_ref.dtype)

    @pl.when(c == nc - 1)
    def _():
        for g in range(SSM_GROUPS):
            hout_ref[g * hp:(g + 1) * hp, :] = h_scr[g].T


def ssd_mixer(z, xbc, dtr, h0, cb0, conv_w, conv_b, dt_bias, a_log, d_skip, norm_g, *, t_valid):
    b, t, _ = z.shape
    L = SSD_CHUNK
    assert t % L == 0
    nc = t // L
    hp_all = SSM_HEADS * SSM_HEAD_DIM
    pad_h = LANES - SSM_HEADS

    def lane_pad(v):
        return jnp.pad(v.astype(f32), (0, pad_h)).reshape(1, LANES)

    cw = jnp.pad(conv_w.astype(f32), ((0, SUBLANES - CONV_W), (0, 0)))
    head_of_col = jnp.arange(hp_all) // SSM_HEAD_DIM
    expand = (jnp.arange(LANES)[:, None] == head_of_col[None, :]).astype(bf16)
    dsk_cols = jnp.repeat(d_skip.astype(f32), SSM_HEAD_DIM).reshape(1, hp_all)
    const = lambda shape: pl.BlockSpec(shape, lambda i, j: (0,) * len(shape))
    return pl.pallas_call(
        functools.partial(_ssd_kernel, chunk=L, t_valid=t_valid),
        out_shape=(jax.ShapeDtypeStruct((b, t, D_INNER), bf16),
                   jax.ShapeDtypeStruct((b, hp_all, D_STATE), f32)),
        grid=(b, nc),
        in_specs=[
            pl.BlockSpec((None, L, D_INNER), lambda i, j: (i, j, 0)),
            pl.BlockSpec((None, L, CONV_DIM), lambda i, j: (i, j, 0)),
            pl.BlockSpec((None, L, LANES), lambda i, j: (i, j, 0)),
            pl.BlockSpec((None, hp_all, D_STATE), lambda i, j: (i, 0, 0)),
            pl.BlockSpec((None, SUBLANES, CONV_DIM), lambda i, j: (i, 0, 0)),
            const((SUBLANES, CONV_DIM)),
            const((1, CONV_DIM)),
            const((1, LANES)), const((1, LANES)), const((1, D_INNER)),
            const((1, D_INNER)),
            const((LANES, hp_all)),
        ],
        out_specs=(pl.BlockSpec((None, L, D_INNER), lambda i, j: (i, j, 0)),
                   pl.BlockSpec((None, hp_all, D_STATE), lambda i, j: (i, 0, 0))),
        scratch_shapes=[pltpu.VMEM((SSM_GROUPS, D_STATE, hp_all // SSM_GROUPS), f32),
                        pltpu.VMEM((2 * SUBLANES, CONV_DIM), f32),
                        pltpu.VMEM((L, CONV_DIM), f32)],
        compiler_params=_cparams(("parallel", "arbitrary")),
        name="ssd_mixer",
    )(z, xbc, dtr, h0, cb0, cw, conv_b.reshape(1, CONV_DIM).astype(f32),
      lane_pad(dt_bias), lane_pad(a_log), dsk_cols, norm_g.reshape(1, D_INNER).astype(f32), expand)


def _sort_network(n):
    def merge(lo, hi, r):
        step = r * 2
        if step < hi - lo:
            yield from merge(lo, hi, step)
            yield from merge(lo + r, hi, step)
            yield from [(i, i + r) for i in range(lo + r, hi - r, step)]
        else:
            yield (lo, lo + r)

    def sort(lo, hi):
        if hi - lo >= 1:
            mid = lo + (hi - lo) // 2
            yield from sort(lo, mid)
            yield from sort(mid + 1, hi)
            yield from merge(lo, hi, 1)

    return list(sort(0, n - 1))


def _sublane_max(x):
    for shift in (4, 2, 1):
        x = jnp.maximum(x, pltpu.roll(x, shift, 0))
    return x


def _pop_columns(cols, extra, n):
    cols = list(cols)
    out = []
    for r in range(n):
        top = cols[0] if extra is None else jnp.maximum(cols[0], extra)
        m = _sublane_max(top)
        out.append(m)
        if r + 1 == n:
            break
        hit = cols[0] == m
        for i in range(min(n - r - 1, len(cols))):
            below = cols[i + 1] if i + 1 < len(cols) else NEG_INF
            cols[i] = jnp.where(hit, below, cols[i])
        if extra is not None:
            extra = jnp.where(extra == m, NEG_INF, extra)
    return out


def _top_values(s, n):
    k = s.shape[0] // SUBLANES
    v = [s[i * SUBLANES:(i + 1) * SUBLANES] for i in range(k)]
    for i, j in _sort_network(k):
        v[i], v[j] = jnp.maximum(v[i], v[j]), jnp.minimum(v[i], v[j])
    return _pop_columns(v[:n], None, n)


def _bf16_pair_words(x):
    hi = pltpu.bitcast(x.astype(bf16).astype(f32), jnp.uint32)
    return hi | (hi >> 16)


def _row_as_bf16_tile(words, rows, zero_words=None):
    tile = jnp.broadcast_to(words, (SUBLANES, words.shape[1]))
    if zero_words is not None:
        tile = tile + zero_words
    packed = pltpu.bitcast(tile, bf16)
    return jnp.concatenate([packed] * (rows // packed.shape[0]), axis=0)


def _peer_scores_kernel(h_ref, g_ref, wq_ref, sk_ref, xnt_ref, rank_ref, n1_ref, c1_ref, e2_ref, q_scr):
    x = h_ref[...]
    ms = jnp.mean(x * x, axis=-1, keepdims=True)
    xn = x * lax.rsqrt(ms + EPS) * g_ref[...]
    xnt_ref[...] = xn.T.astype(bf16)
    q = jnp.dot(xn.astype(bf16), wq_ref[...], preferred_element_type=f32).astype(bf16)
    for i in range(2 * PEER_HEADS):
        q_scr[i] = q[:, i * PEER_HALF:(i + 1) * PEER_HALF]
    kk = PEER_TOPK
    sk0 = sk_ref[0]
    sk1 = sk_ref[1]

    nkeys = sk0.shape[0]
    tokens = h_ref.shape[0]

    def rep(tile):
        return jnp.concatenate([tile] * (nkeys // SUBLANES), axis=0)

    def select_half(s1, s2):
        av = _top_values(s1, kk)
        bv = _top_values(s2, kk)
        sub = lax.broadcasted_iota(jnp.int32, av[0].shape, 0)
        a_lo, a_hi = av[SUBLANES - 1], av[2 * SUBLANES - 1]
        for r in range(SUBLANES - 2, -1, -1):
            a_lo = jnp.where(sub == r, av[r], a_lo)
            a_hi = jnp.where(sub == r, av[SUBLANES + r], a_hi)
        cand = [a_lo + bv[0]]
        for c in range(1, kk):
            cand.append(jnp.where(sub < kk // (c + 1), a_lo + bv[c], NEG_INF))
        best = _pop_columns(cand, a_hi + bv[0], kk)
        tau, top = best[kk - 1], best[0]
        zsum = jnp.zeros_like(top)
        for r in range(kk):
            zsum = zsum + jnp.exp(best[r] - top)
        tau_r = rep(tau)
        rank2 = jnp.full(s2.shape, float(kk), f32)
        n1 = jnp.zeros(s1.shape, f32)
        for c in range(kk - 1, -1, -1):
            b_c = rep(bv[c])
            rank2 = jnp.where(s2 >= b_c, float(c), rank2)
            n1 = n1 + jnp.where(s1 + b_c >= tau_r, 1.0, 0.0)
        c1 = jnp.exp(s1 - rep(av[0])) / rep(zsum)
        e2 = jnp.exp(s2 - rep(bv[0]))
        return rank2, n1, c1, e2

    def head(h, carry):
        s1 = _dot_nt(sk0, q_scr[2 * h])
        s2 = _dot_nt(sk1, q_scr[2 * h + 1])
        parts = [select_half(s1[:, l0:l0 + LANES], s2[:, l0:l0 + LANES]) for l0 in range(0, tokens, LANES)]
        rank2, n1, c1, e2 = (jnp.concatenate(p, axis=1) for p in zip(*parts))
        rank_ref[h] = rank2.astype(rank_ref.dtype)
        n1_ref[h] = _bf16_pair_words(n1)
        c1_ref[h] = _bf16_pair_words(c1)
        e2_ref[h] = e2.astype(e2_ref.dtype)
        return carry

    lax.fori_loop(0, PEER_HEADS, head, 0, unroll=True)


def peer_scores(h, gain, wq, sk, *, tm):
    n, d = h.shape
    nq = wq.shape[1]
    sel16 = jax.ShapeDtypeStruct((PEER_HEADS, PEER_N_KEYS, n), bf16)
    sel32 = jax.ShapeDtypeStruct((PEER_HEADS, PEER_N_KEYS, n), jnp.uint32)
    sel_spec = pl.BlockSpec((PEER_HEADS, PEER_N_KEYS, tm), lambda i: (0, 0, i))
    return pl.pallas_call(
        _peer_scores_kernel,
        out_shape=(jax.ShapeDtypeStruct((d, n), bf16), sel16, sel32, sel32, sel16),
        grid=(n // tm,),
        in_specs=[pl.BlockSpec((tm, d), lambda i: (i, 0)),
                  pl.BlockSpec((1, d), lambda i: (0, 0)),
                  pl.BlockSpec((d, nq), lambda i: (0, 0)),
                  pl.BlockSpec((2, PEER_N_KEYS, PEER_HALF), lambda i: (0, 0, 0))],
        out_specs=(pl.BlockSpec((d, tm), lambda i: (0, i)), sel_spec, sel_spec, sel_spec, sel_spec),
        scratch_shapes=[pltpu.VMEM((2 * PEER_HEADS, tm, PEER_HALF), bf16)],
        compiler_params=_cparams(("parallel",)),
        name="peer_scores",
    )(h, gain.reshape(1, d).astype(f32), wq, sk)


def _peer_dense_kernel(xnt_ref, u_ref, vt_ref, rank_ref, n1_ref, c1_ref, e2_ref, h_ref, o_ref,
                       acc_scr, coef_a, coef_b, *, te, sub, ne):
    j = pl.program_id(1)
    nk = PEER_N_KEYS
    a_per_step = te // nk
    a_per_sub = sub // nk
    tm = xnt_ref.shape[1]
    pk = 2 * SUBLANES
    zero = jnp.zeros((), bf16)

    def drain(src, part, parts):
        d = acc_scr.shape[0]
        rows = slice(part * d // parts, (part + 1) * d // parts)
        upd = jnp.dot(vt_ref[rows, :], src[...], preferred_element_type=f32)
        acc_scr[rows, :] += upd
        bits = pltpu.bitcast(upd[0:SUBLANES, :], jnp.uint32)
        return lax.shift_right_logical(bits, jnp.uint32(32))

    def build(dst, src):
        xnt = xnt_ref[...]
        nsub = te // sub

        def scores(s):
            return jnp.dot(u_ref[s * sub:(s + 1) * sub, :], xnt, preferred_element_type=f32)

        def masks(act, s, token):
            for ai in range(a_per_sub):
                ar = s * a_per_sub + ai
                grp = j * (a_per_step // SUBLANES) + ar // SUBLANES
                row = ar % SUBLANES
                wsum = None
                for h in range(PEER_HEADS):
                    tie = token if (h == 0 and token is not None) else None
                    n1a = _row_as_bf16_tile(n1_ref[h, grp, row:row + 1, :], nk, tie)
                    c1a = _row_as_bf16_tile(c1_ref[h, grp, row:row + 1, :], nk)
                    term = jnp.where(rank_ref[h] < n1a, e2_ref[h], zero) * c1a
                    wsum = term if wsum is None else wsum + term
                g = _gelu_tanh(act[ai * nk:(ai + 1) * nk].astype(bf16))
                dst[ar * nk:(ar + 1) * nk, :] = wsum * g

        token = None
        for s in range(nsub):
            act = scores(s)
            nxt = drain(src, s, nsub) if src is not None else None
            masks(act, s, token)
            token = nxt

    @pl.when(j == 0)
    def _():
        acc_scr[...] = jnp.zeros_like(acc_scr)
        build(coef_a, None)

    @pl.when(jnp.logical_and(jnp.logical_and(j > 0, j < ne), j % 2 == 0))
    def _():
        build(coef_a, coef_b)

    @pl.when(jnp.logical_and(j < ne, j % 2 == 1))
    def _():
        build(coef_b, coef_a)

    @pl.when(j == ne)
    def _():
        drain(coef_b if ne % 2 == 0 else coef_a, 0, 1)
        o_ref[...] = h_ref[...] + acc_scr[...].T


def peer_dense(xnt, u, vt, rank2, n1, c1, e2, h, *, tm, te, sub):
    n, d = h.shape
    n_exp = u.shape[0]
    ne = n_exp // te
    assert (te // PEER_N_KEYS) % SUBLANES == 0
    sel_spec = pl.BlockSpec((PEER_HEADS, PEER_N_KEYS, tm), lambda i, j: (0, 0, i))
    ngrp = PEER_N_KEYS // SUBLANES
    n1 = n1.reshape(PEER_HEADS, ngrp, SUBLANES, n)
    c1 = c1.reshape(PEER_HEADS, ngrp, SUBLANES, n)
    row_spec = pl.BlockSpec((PEER_HEADS, ngrp, SUBLANES, tm), lambda i, j: (0, 0, 0, i))
    return pl.pallas_call(
        functools.partial(_peer_dense_kernel, te=te, sub=sub, ne=ne),
        out_shape=jax.ShapeDtypeStruct((n, d), f32),
        grid=(n // tm, ne + 1),
        in_specs=[pl.BlockSpec((d, tm), lambda i, j: (0, i)),
                  pl.BlockSpec((te, d), lambda i, j: (jnp.minimum(j, ne - 1), 0)),
                  pl.BlockSpec((d, te), lambda i, j: (0, jnp.maximum(j - 1, 0))),
                  sel_spec, row_spec, row_spec, sel_spec,
                  pl.BlockSpec((tm, d), lambda i, j: (i, 0))],
        out_specs=pl.BlockSpec((tm, d), lambda i, j: (i, 0)),
        scratch_shapes=[pltpu.VMEM((d, tm), f32), pltpu.VMEM((te, tm), bf16), pltpu.VMEM((te, tm), bf16)],
        compiler_params=_cparams(("parallel", "arbitrary")),
        name="peer_dense",
    )(xnt, u, vt, rank2, n1, c1, e2, h)


ATTN_BLOCK = 128


def _attn_band_kernel(*refs, dil):
    nq = Q_PER_KV * KV_PER_BRANCH * HEAD_DIM // LANES
    q_refs = refs[:nq]
    kc_ref, kp_ref, vc_ref, vp_ref, o_ref, l_ref, o_scr, l_scr = refs[nq:]
    i = pl.program_id(1)
    blk = ATTN_BLOCK
    scale = HEAD_DIM ** -0.5
    rows = Q_PER_KV * blk
    a_i = lax.broadcasted_iota(jnp.int32, (rows, blk), 0) % blk
    c_i = lax.broadcasted_iota(jnp.int32, (rows, blk), 1)
    mask_c = c_i <= a_i
    mask_p = jnp.logical_and(c_i >= a_i, i > 0)

    def residue(r, carry):
        sel = pl.ds(r, blk, stride=dil) if dil > 1 else pl.ds(0, blk)
        q = jnp.concatenate([qr[sel, :] for qr in q_refs], axis=1) * scale
        kc, kp, vc, vp = kc_ref[sel, :], kp_ref[sel, :], vc_ref[sel, :], vp_ref[sel, :]
        outs, lses = [], []
        for j in range(KV_PER_BRANCH):
            hs = slice(j * HEAD_DIM, (j + 1) * HEAD_DIM)
            qs = jnp.concatenate([q[:, (j * Q_PER_KV + x) * HEAD_DIM:(j * Q_PER_KV + x + 1) * HEAD_DIM]
                                  for x in range(Q_PER_KV)], axis=0).astype(bf16)
            s_c = jnp.where(mask_c, _dot_nt(qs, kc[:, hs].astype(bf16)), NEG_INF)
            s_p = jnp.where(mask_p, _dot_nt(qs, kp[:, hs].astype(bf16)), NEG_INF)
            m = jnp.maximum(jnp.max(s_c, axis=-1, keepdims=True), jnp.max(s_p, axis=-1, keepdims=True))
            p_c = jnp.exp(s_c - m)
            p_p = jnp.exp(s_p - m)
            l = jnp.sum(p_c, axis=-1, keepdims=True) + jnp.sum(p_p, axis=-1, keepdims=True)
            o = jnp.dot(p_c.astype(bf16), vc[:, hs].astype(bf16), preferred_element_type=f32)
            o = o + jnp.dot(p_p.astype(bf16), vp[:, hs].astype(bf16), preferred_element_type=f32)
            o = o / l
            lse = jnp.broadcast_to(m + jnp.log(l), (rows, HEAD_DIM))
            for x in range(Q_PER_KV):
                outs.append(o[x * blk:(x + 1) * blk])
                lses.append(lse[x * blk:(x + 1) * blk])
        per = LANES // HEAD_DIM
        for cb in range(nq):
            o_scr[cb, sel, :] = jnp.concatenate(outs[cb * per:(cb + 1) * per], axis=1)
            l_scr[cb, sel, :] = jnp.concatenate(lses[cb * per:(cb + 1) * per], axis=1)
        return carry

    lax.fori_loop(0, dil, residue, 0)
    for cb in range(nq):
        o_ref[:, cb * LANES:(cb + 1) * LANES] = o_scr[cb]
        l_ref[:, cb * LANES:(cb + 1) * LANES] = l_scr[cb]


def attn_band(q, k, v, g, dil):
    b, t, _ = q.shape
    span = ATTN_BLOCK * dil
    assert t % span == 0
    hq = Q_PER_KV * KV_PER_BRANCH * HEAD_DIM
    hk = KV_PER_BRANCH * HEAD_DIM
    cur = lambda bi, i: (bi, i, g)
    prev = lambda bi, i: (bi, jnp.maximum(i - 1, 0), g)
    osd = jax.ShapeDtypeStruct((b, t, hq), f32)
    nq = hq // LANES
    q_specs = [pl.BlockSpec((None, span, LANES), functools.partial(lambda bi, i, cb: (bi, i, g * nq + cb), cb=cb))
               for cb in range(nq)]
    return pl.pallas_call(
        functools.partial(_attn_band_kernel, dil=dil),
        out_shape=(osd, osd),
        grid=(b, t // span),
        in_specs=q_specs + [pl.BlockSpec((None, span, hk), cur), pl.BlockSpec((None, span, hk), prev),
                            pl.BlockSpec((None, span, hk), cur), pl.BlockSpec((None, span, hk), prev)],
        out_specs=(pl.BlockSpec((None, span, hq), lambda bi, i: (bi, i, 0)),
                   pl.BlockSpec((None, span, hq), lambda bi, i: (bi, i, 0))),
        scratch_shapes=[pltpu.VMEM((nq, span, LANES), f32), pltpu.VMEM((nq, span, LANES), f32)],
        compiler_params=_cparams(("parallel", "arbitrary")),
        name=f"attn_band_d{dil}",
    )(*([q] * nq), k, k, v, v)


def _merge3_kernel(o0, o1, o2, l0, l1, l2, out_ref):
    a0, a1, a2 = l0[...], l1[...], l2[...]
    m = jnp.maximum(jnp.maximum(a0, a1), a2)
    w0, w1, w2 = jnp.exp(a0 - m), jnp.exp(a1 - m), jnp.exp(a2 - m)
    out_ref[...] = ((w0 * o0[...] + w1 * o1[...] + w2 * o2[...]) / (w0 + w1 + w2)).astype(out_ref.dtype)


def merge3(os_, ls_, *, tm):
    n, c = os_[0].shape
    spec = pl.BlockSpec((tm, c), lambda i: (i, 0))
    return pl.pallas_call(
        _merge3_kernel,
        out_shape=jax.ShapeDtypeStruct((n, c), bf16),
        grid=(n // tm,),
        in_specs=[spec] * 6,
        out_specs=spec,
        compiler_params=_cparams(("parallel",)),
        name="attn_merge",
    )(*os_, *ls_)


def _attn_sample_kernel(q_ref, ck_ref, cv_ref, nk_ref, nv_ref, o_ref, *, t_new, win_buf):
    scale = HEAD_DIM ** -0.5
    q = q_ref[...] * scale
    n_cache = ck_ref.shape[0]
    n_new = nk_ref.shape[0]
    rows = Q_PER_KV * t_new
    t_c = lax.broadcasted_iota(jnp.int32, (rows, n_cache), 0) % t_new
    d_c = win_buf + t_c - lax.broadcasted_iota(jnp.int32, (rows, n_cache), 1)
    t_n = lax.broadcasted_iota(jnp.int32, (rows, n_new), 0) % t_new
    d_n = t_n - lax.broadcasted_iota(jnp.int32, (rows, n_new), 1)
    outs = [None] * (KV_PER_BRANCH * Q_PER_KV)
    for j in range(KV_PER_BRANCH):
        parts = []
        for g, (win, dil) in enumerate(DIL_PATTERNS):
            hq = (g * KV_PER_BRANCH + j) * Q_PER_KV
            hk = slice((g * KV_PER_BRANCH + j) * HEAD_DIM, (g * KV_PER_BRANCH + j + 1) * HEAD_DIM)
            qs = jnp.concatenate([q[:, (hq + r) * HEAD_DIM:(hq + r + 1) * HEAD_DIM] for r in range(Q_PER_KV)],
                                 axis=0).astype(bf16)
            ok_c = (d_c >= 0) & (d_c <= win) & ((d_c & (dil - 1)) == 0)
            ok_n = (d_n >= 0) & (d_n <= win) & ((d_n & (dil - 1)) == 0)
            s_c = jnp.where(ok_c, _dot_nt(qs, ck_ref[:, hk].astype(bf16)), NEG_INF)
            s_n = jnp.where(ok_n, _dot_nt(qs, nk_ref[:, hk].astype(bf16)), NEG_INF)
            parts.append((s_c, s_n, hk))
        m = None
        for s_c, s_n, _ in parts:
            mm = jnp.maximum(jnp.max(s_c, axis=-1, keepdims=True), jnp.max(s_n, axis=-1, keepdims=True))
            m = mm if m is None else jnp.maximum(m, mm)
        l = jnp.zeros_like(m)
        o = jnp.zeros((rows, HEAD_DIM), f32)
        for s_c, s_n, hk in parts:
            p_c = jnp.exp(s_c - m)
            p_n = jnp.exp(s_n - m)
            l = l + jnp.sum(p_c, axis=-1, keepdims=True) + jnp.sum(p_n, axis=-1, keepdims=True)
            o = o + jnp.dot(p_c.astype(bf16), cv_ref[:, hk].astype(bf16), preferred_element_type=f32)
            o = o + jnp.dot(p_n.astype(bf16), nv_ref[:, hk].astype(bf16), preferred_element_type=f32)
        o = o / l
        for r in range(Q_PER_KV):
            outs[j * Q_PER_KV + r] = o[r * t_new:(r + 1) * t_new]
    o_ref[...] = jnp.concatenate(outs, axis=1).astype(o_ref.dtype)


def attn_sample(q, ck, cv, nk, nv):
    b, t_new, qc = q.shape
    w = ck.shape[1]
    kc = ck.shape[2]
    n_new = nk.shape[1]
    return pl.pallas_call(
        functools.partial(_attn_sample_kernel, t_new=t_new, win_buf=w),
        out_shape=jax.ShapeDtypeStruct((b, t_new, ATTN_OUT_DIM), bf16),
        grid=(b,),
        in_specs=[pl.BlockSpec((None, t_new, qc), lambda i: (i, 0, 0)),
                  pl.BlockSpec((None, w, kc), lambda i: (i, 0, 0)),
                  pl.BlockSpec((None, w, kc), lambda i: (i, 0, 0)),
                  pl.BlockSpec((None, n_new, kc), lambda i: (i, 0, 0)),
                  pl.BlockSpec((None, n_new, kc), lambda i: (i, 0, 0))],
        out_specs=pl.BlockSpec((None, t_new, ATTN_OUT_DIM), lambda i: (i, 0, 0)),
        compiler_params=_cparams(("parallel",)),
        name="attn_sample",
    )(q, ck, cv, nk, nv)


def _prep_weights(norm_mix, norm_ffn, norm_ple, ssm_w_in, ssm_conv_w, ssm_conv_b, ssm_dt_bias, ssm_a_log, ssm_d,
                  ssm_norm, ssm_w_out, kv_norm, w_kv, k_norm, w_q, q_norm, w_o,
                  peer_w_query, peer_sub_keys, peer_u, peer_v, ple_w_proj, ple_w_gate):
    zx = D_INNER + CONV_DIM
    w_in = ssm_w_in[0]
    kdim = N_KV_HEADS * HEAD_DIM
    rep = LANES // HEAD_DIM
    return dict(
        norm_mix=norm_mix, norm_ffn=norm_ffn, norm_ple=norm_ple,
        w_z=w_in[:, :D_INNER].astype(bf16), w_xbc=w_in[:, D_INNER:zx].astype(bf16),
        w_dt=jnp.pad(w_in[:, zx:], ((0, 0), (0, LANES - SSM_HEADS))).astype(bf16),
        conv_w=ssm_conv_w[0], conv_b=ssm_conv_b[0], dt_bias=ssm_dt_bias[0], a_log=ssm_a_log[0],
        d_skip=ssm_d[0], ssm_norm=ssm_norm[0], w_out=ssm_w_out[0].astype(bf16),
        kv_norm=kv_norm, w_k=w_kv[:, :kdim].astype(bf16), w_v=w_kv[:, kdim:].astype(bf16),
        k_gain=jnp.tile(k_norm.astype(f32), rep).reshape(1, LANES),
        w_q=w_q[0].astype(bf16), q_gain=jnp.tile(q_norm[0].astype(f32), rep).reshape(1, LANES),
        w_o=w_o[0].astype(bf16),
        peer_wq=[peer_w_query[i].astype(bf16) for i in range(DEPTH)],
        peer_sk=[peer_sub_keys[i].astype(bf16) for i in range(DEPTH)],
        peer_u=[peer_u[i].astype(bf16) for i in range(DEPTH)],
        peer_vt=[peer_v[i].astype(bf16).T for i in range(DEPTH)],
        ple_wp=[ple_w_proj[i].astype(bf16) for i in range(DEPTH)],
        ple_wg=[ple_w_gate[i].astype(bf16) for i in range(DEPTH)],
    )


PEER_TOKEN_TILE = 512
PEER_SCORE_TILE = 256
PEER_EXPERT_TILE = 2 * SUBLANES * PEER_N_KEYS
PEER_SUB_TILE = 1024


def _peer_ple(h, p, i, wts, tm):
    n = h.shape[0]
    tm_d = PEER_TOKEN_TILE if n % PEER_TOKEN_TILE == 0 else n
    xnt, rank2, n1, c1, e2 = peer_scores(h, wts["norm_ffn"][i], wts["peer_wq"][i], wts["peer_sk"][i],
                                         tm=PEER_SCORE_TILE)
    h = peer_dense(xnt, wts["peer_u"][i], wts["peer_vt"][i], rank2, n1, c1, e2, h,
                   tm=tm_d, te=PEER_EXPERT_TILE, sub=PEER_SUB_TILE)
    return fused_matmul(h, wts["ple_wg"][i], gain=wts["norm_ple"][i], mode="ple", res=h,
                        ple=(p, wts["ple_wp"][i]), tm=tm)


def _trunk(x, p, conv_state, ssm_state, past_k, past_v, pos0, wts):
    b, t, d = x.shape
    n = b * t
    prompt = past_k is None
    tm = _token_tile(n)
    h = x.reshape(n, d)

    z = fused_matmul(h, wts["w_z"], gain=wts["norm_mix"][0], tm=tm).reshape(b, t, D_INNER)
    xbc = fused_matmul(h, wts["w_xbc"], gain=wts["norm_mix"][0], tm=tm).reshape(b, t, CONV_DIM)
    dtr = fused_matmul(h, wts["w_dt"], gain=wts["norm_mix"][0], tm=tm)
    new_conv = xbc[:, t - (CONV_W - 1):, :] if t >= CONV_W - 1 else None
    dtr = dtr.reshape(b, t, LANES)
    tp = -(-t // SSD_CHUNK) * SSD_CHUNK
    if tp != t:
        padt = ((0, 0), (0, tp - t), (0, 0))
        z_p, xbc_p, dtr_p = jnp.pad(z, padt), jnp.pad(xbc, padt), jnp.pad(dtr, padt)
    else:
        z_p, xbc_p, dtr_p = z, xbc, dtr
    cb0 = jnp.pad(conv_state.astype(f32), ((0, 0), (SUBLANES - (CONV_W - 1), 0), (0, 0)))
    h0 = ssm_state.astype(f32).reshape(b, SSM_HEADS * SSM_HEAD_DIM, D_STATE)
    y, h_fin = ssd_mixer(z_p, xbc_p, dtr_p, h0, cb0, wts["conv_w"], wts["conv_b"], wts["dt_bias"], wts["a_log"],
                         wts["d_skip"], wts["ssm_norm"], t_valid=(None if tp == t else t))
    y = y[:, :t].reshape(n, D_INNER)
    h = fused_matmul(y, wts["w_out"], mode="add", res=h, tm=tm)
    new_ssm = h_fin.reshape(b, SSM_HEADS, SSM_HEAD_DIM, D_STATE)
    h = _peer_ple(h, p[0].reshape(n, -1), 0, wts, tm)

    pos = pos0 + jnp.arange(t, dtype=jnp.int32)
    tabs = _rope_tables(pos)
    if t < tm:
        tabs = tuple(jnp.tile(a, (tm // t, 1)) for a in tabs)
    k_new = fused_matmul(h, wts["w_k"], gain=wts["kv_norm"], mode="headrope",
                         headrope=(wts["k_gain"],) + tabs, tm=tm)
    v_new = fused_matmul(h, wts["w_v"], gain=wts["kv_norm"], tm=tm)

    q = fused_matmul(h, wts["w_q"], gain=wts["norm_mix"][1], mode="headrope",
                     headrope=(wts["q_gain"],) + tabs, tm=tm)
    kdim = N_KV_HEADS * HEAD_DIM
    if prompt:
        q3, k3, v3 = q.reshape(b, t, -1), k_new.reshape(b, t, kdim), v_new.reshape(b, t, kdim)
        os_, ls_ = [], []
        for g, (_, dil) in enumerate(DIL_PATTERNS):
            o_g, l_g = attn_band(q3, k3, v3, g, dil)
            os_.append(o_g.reshape(n, ATTN_OUT_DIM))
            ls_.append(l_g.reshape(n, ATTN_OUT_DIM))
        att = merge3(os_, ls_, tm=tm)
    else:
        padn = ((0, 0), (0, SSD_CHUNK - t), (0, 0))
        nk = jnp.pad(k_new.reshape(b, t, kdim), padn)
        nv = jnp.pad(v_new.reshape(b, t, kdim), padn)
        w = past_k.shape[1]
        att = attn_sample(q.reshape(b, t, -1), past_k.reshape(b, w, kdim).astype(f32),
                          past_v.reshape(b, w, kdim).astype(f32), nk, nv).reshape(n, ATTN_OUT_DIM)
    h = fused_matmul(att, wts["w_o"], mode="add", res=h, tm=tm)
    h = _peer_ple(h, p[1].reshape(n, -1), 1, wts, tm)

    return (h.reshape(b, t, d), new_conv[None], new_ssm[None],
            k_new.reshape(b, t, N_KV_HEADS, HEAD_DIM), v_new.reshape(b, t, N_KV_HEADS, HEAD_DIM))


def kernel(x_prompt, x_sample, state_conv, state_ssm, cache_k, cache_v, p_prompt, p_sample, norm_mix, norm_ffn, norm_ple, ssm_w_in, ssm_conv_w, ssm_conv_b, ssm_dt_bias, ssm_a_log, ssm_d, ssm_norm, ssm_w_out, kv_norm, w_kv, k_norm, w_q, q_norm, w_o, peer_w_query, peer_sub_keys, peer_u, peer_v, ple_w_proj, ple_w_gate):
    wts = _prep_weights(norm_mix, norm_ffn, norm_ple, ssm_w_in, ssm_conv_w, ssm_conv_b, ssm_dt_bias, ssm_a_log,
                        ssm_d, ssm_norm, ssm_w_out, kv_norm, w_kv, k_norm, w_q, q_norm, w_o,
                        peer_w_query, peer_sub_keys, peer_u, peer_v, ple_w_proj, ple_w_gate)
    b_p, t_p, _ = x_prompt.shape
    zero_conv = jnp.zeros((b_p, CONV_W - 1, CONV_DIM), x_prompt.dtype)
    zero_ssm = jnp.zeros((b_p, SSM_HEADS, SSM_HEAD_DIM, D_STATE), x_prompt.dtype)
    y_p, conv_p, ssm_p, k_p, v_p = _trunk(x_prompt, p_prompt, zero_conv, zero_ssm, None, None, 0, wts)
    keep = min(max(w for w, _ in DIL_PATTERNS), t_p)
    past_len = 16384
    y_s, conv_s, ssm_s, k_s, v_s = _trunk(x_sample, p_sample, state_conv[0], state_ssm[0], cache_k, cache_v,
                                          past_len, wts)
    if keep < t_p:
        k_p, v_p = k_p[:, -keep:], v_p[:, -keep:]
    return (y_p, y_s, conv_p, ssm_p, k_p, v_p, conv_s, ssm_s, k_s, v_s)
```

```python
import functools
import math

import jax
import jax.numpy as jnp
from jax import lax
from jax.experimental import pallas as pl
from jax.experimental.pallas import tpu as pltpu

f32 = jnp.float32
bf16 = jnp.bfloat16

D_MODEL = 1024
DEPTH = 2
N_A_LAYERS = 1
D_INNER = 2048
SSM_HEAD_DIM = 64
SSM_HEADS = 32
SSM_GROUPS = 8
SSM_HEADS_PER_GROUP = 4
D_STATE = 128
CONV_W = 4
CONV_DIM = 4096
SSD_CHUNK = 128
HEAD_DIM = 64
DIL_PATTERNS = ((128, 1), (512, 4), (2048, 16))
N_DIL = 3
KV_PER_BRANCH = 2
Q_PER_KV = 4
N_Q_HEADS = 24
N_KV_HEADS = 6
ATTN_OUT_DIM = 512
ROPE_DIM = 16
ROPE_THETA = 500000.0
PEER_HEADS = 8
PEER_N_KEYS = 128
PEER_TOPK = 16
PEER_HALF = 128
EPS = 1e-6

LANES = 128
SUBLANES = 8
VMEM_LIMIT_BYTES = 56 * 1024 * 1024

NEG_INF = float("-inf")


def _cparams(sem):
    return pltpu.CompilerParams(dimension_semantics=sem, vmem_limit_bytes=VMEM_LIMIT_BYTES)


def _dot_nt(a, b):
    return lax.dot_general(a, b, (((1,), (1,)), ((), ())), preferred_element_type=f32)


def _dot_tn(a, b):
    return lax.dot_general(a, b, (((0,), (0,)), ((), ())), preferred_element_type=f32)


def _sigmoid(x):
    return 1.0 / (1.0 + jnp.exp(-x))


def _silu(x):
    return x * _sigmoid(x)


def _softplus(x):
    return jnp.maximum(x, 0.0) + jnp.log1p(jnp.exp(-jnp.abs(x)))


def _gelu_tanh(x):
    c = math.sqrt(2.0 / math.pi)
    return x * (0.5 + 0.5 * jnp.tanh(x * (c + (c * 0.044715) * (x * x))))


def _fmm_kernel(*refs, has_norm, mode):
    it = iter(refs)
    x_ref = next(it)
    g_ref = next(it) if has_norm else None
    w_ref = next(it)
    res_ref = next(it) if mode in ("add", "ple") else None
    if mode == "ple":
        p_ref, wp_ref = next(it), next(it)
    if mode == "headrope":
        hg_ref, ta_ref, tb_ref, tc_ref, bd_ref = next(it), next(it), next(it), next(it), next(it)
    o_ref = next(it)
    xn_ref = next(it)

    @pl.when(pl.program_id(1) == 0)
    def _():
        x = x_ref[...].astype(f32)
        if has_norm:
            ms = jnp.mean(x * x, axis=-1, keepdims=True)
            x = x * lax.rsqrt(ms + EPS) * g_ref[...]
        xn_ref[...] = x.astype(bf16)

    acc = jnp.dot(xn_ref[...], w_ref[...], preferred_element_type=f32)
    if mode == "plain":
        o_ref[...] = acc.astype(o_ref.dtype)
    elif mode == "add":
        o_ref[...] = res_ref[...] + acc
    elif mode == "ple":
        pp = jnp.dot(p_ref[...].astype(bf16), wp_ref[...], preferred_element_type=f32)
        o_ref[...] = res_ref[...] + pp * _sigmoid(acc)
    elif mode == "headrope":
        tn = acc.shape[1]
        ta, tb, tc = ta_ref[...], tb_ref[...], tc_ref[...]
        hg = hg_ref[...]
        bd = bd_ref[...]
        for nb in range(tn // LANES):
            blk = acc[:, nb * LANES:(nb + 1) * LANES]
            sq = blk * blk
            sq_hi = sq.astype(bf16)
            sq_lo = (sq - sq_hi.astype(f32)).astype(bf16)
            ms = (jnp.dot(sq_hi, bd, preferred_element_type=f32) + jnp.dot(sq_lo, bd, preferred_element_type=f32))
            y = blk * lax.rsqrt(ms + EPS) * hg
            y = y * ta + pltpu.roll(y, LANES - ROPE_DIM // 2, 1) * tb + pltpu.roll(y, ROPE_DIM // 2, 1) * tc
            o_ref[:, nb * LANES:(nb + 1) * LANES] = y


MATMUL_VMEM_BUDGET = 40 * 1024 * 1024


def _token_tile(m):
    return next((c for c in (1024, 512, 256, 128) if m % c == 0), m)


def _col_tile(tm, k, n, x_bytes, mode, kp):
    out_blocks = 2 if mode in ("add", "ple") else 1
    for tn in sorted({n, 2048, 1536, 1024, 768, 512, 384, 256, 128}, reverse=True):
        if tn > n or n % tn:
            continue
        use = 2 * tm * k * x_bytes + tm * k * 2 + 2 * k * tn * 2 + 2 * out_blocks * tm * tn * 4
        if mode == "ple":
            use += 2 * tm * kp * 4 + 2 * kp * tn * 2
        if use <= MATMUL_VMEM_BUDGET:
            return tn
    raise ValueError("no column tile fits")


def fused_matmul(x, w, *, gain=None, mode="plain", res=None, ple=None, headrope=None, tm, tn=None, out_dtype=f32):
    m, k = x.shape
    n = w.shape[1]
    if tn is None:
        tn = _col_tile(tm, k, n, x.dtype.itemsize, mode, ple[0].shape[1] if ple else 0)
    assert m % tm == 0 and n % tn == 0, (m, tm, n, tn)
    has_norm = gain is not None
    args = [x]
    in_specs = [pl.BlockSpec((tm, k), lambda i, j: (i, 0))]
    if has_norm:
        args.append(gain.reshape(1, k).astype(f32))
        in_specs.append(pl.BlockSpec((1, k), lambda i, j: (0, 0)))
    args.append(w)
    in_specs.append(pl.BlockSpec((k, tn), lambda i, j: (0, j)))
    if mode in ("add", "ple"):
        args.append(res)
        in_specs.append(pl.BlockSpec((tm, tn), lambda i, j: (i, j)))
    if mode == "ple":
        p, wp = ple
        kp = p.shape[1]
        args += [p, wp]
        in_specs += [pl.BlockSpec((tm, kp), lambda i, j: (i, 0)), pl.BlockSpec((kp, tn), lambda i, j: (0, j))]
    if mode == "headrope":
        hg, ta, tb, tc = headrope
        period = ta.shape[0] // tm
        args += [hg, ta, tb, tc, _head_mean_matrix()]
        in_specs.append(pl.BlockSpec((1, LANES), lambda i, j: (0, 0)))
        for _ in range(3):
            in_specs.append(pl.BlockSpec((tm, LANES), lambda i, j: (i % period, 0)))
        in_specs.append(pl.BlockSpec((LANES, LANES), lambda i, j: (0, 0)))
    return pl.pallas_call(
        functools.partial(_fmm_kernel, has_norm=has_norm, mode=mode),
        out_shape=jax.ShapeDtypeStruct((m, n), out_dtype),
        grid=(m // tm, n // tn),
        in_specs=in_specs,
        out_specs=pl.BlockSpec((tm, tn), lambda i, j: (i, j)),
        scratch_shapes=[pltpu.VMEM((tm, k), bf16)],
        compiler_params=_cparams(("parallel", "arbitrary")),
        name="fused_matmul_" + mode,
    )(*args)


def _head_mean_matrix():
    r = jnp.arange(LANES) // HEAD_DIM
    return ((r[:, None] == r[None, :]).astype(f32) / HEAD_DIM).astype(bf16)


def _rope_tables(pos):
    half = ROPE_DIM // 2
    inv = ROPE_THETA ** (-jnp.arange(0, ROPE_DIM, 2, dtype=f32) / ROPE_DIM)
    ang = pos.astype(f32)[:, None] * inv[None, :]
    cos, sin = jnp.cos(ang), jnp.sin(ang)
    t = pos.shape[0]
    ones = jnp.ones((t, HEAD_DIM - ROPE_DIM), f32)
    zeros_h = jnp.zeros((t, half), f32)
    zeros_r = jnp.zeros((t, HEAD_DIM - ROPE_DIM), f32)
    ta = jnp.concatenate([cos, cos, ones], axis=1)
    tb = jnp.concatenate([-sin, zeros_h, zeros_r], axis=1)
    tc = jnp.concatenate([zeros_h, sin, zeros_r], axis=1)
    rep = LANES // HEAD_DIM
    return tuple(jnp.tile(a, (1, rep)) for a in (ta, tb, tc))


def _ssd_kernel(z_ref, xbc_ref, dt_ref, h0_ref, cb0_ref, cw_ref, cbias_ref, dtb_ref, alog_ref, dsk_ref, ng_ref,
                ex_ref, y_ref, hout_ref, h_scr, tail_scr, xc_scr, *, chunk, t_valid):
    c = pl.program_id(1)
    nc = pl.num_programs(1)
    L = chunk
    hp = SSM_HEADS_PER_GROUP * SSM_HEAD_DIM

    @pl.when(c == 0)
    def _():
        for g in range(SSM_GROUPS):
            h_scr[g] = h0_ref[g * hp:(g + 1) * hp, :].T
        tail_scr[0:SUBLANES, :] = cb0_ref[...]

    w = cw_ref[...]
    tail_scr[SUBLANES:2 * SUBLANES, :] = xbc_ref[0:SUBLANES, :]
    for lo, hi, src, off in ((0, SUBLANES, tail_scr, SUBLANES), (SUBLANES, L, xbc_ref, 0)):
        acc = src[lo + off:hi + off, :] * w[CONV_W - 1:CONV_W, :] + cbias_ref[...]
        for s in range(1, CONV_W):
            acc = acc + src[lo + off - s:hi + off - s, :] * w[CONV_W - 1 - s:CONV_W - s, :]
        xc_scr[lo:hi, :] = _silu(acc)
    tail_scr[0:SUBLANES, :] = xbc_ref[L - SUBLANES:L, :]

    dt = _softplus(dt_ref[...] + dtb_ref[...])
    if t_valid is not None:
        rowid = c * L + lax.broadcasted_iota(jnp.int32, (L, LANES), 0)
        dt = jnp.where(rowid < t_valid, dt, 0.0)
    a = -jnp.exp(alog_ref[...])
    da = dt * a
    r_i = lax.broadcasted_iota(jnp.int32, (L, L), 0)
    c_i = lax.broadcasted_iota(jnp.int32, (L, L), 1)
    causal = r_i >= c_i
    cum = jnp.dot(causal.astype(f32), da, precision=lax.Precision.HIGHEST, preferred_element_type=f32)
    cum_t = cum.T
    cum_last = cum[L - 1:L, :]
    tailw = jnp.exp(cum_last - cum) * dt
    ecum = jnp.exp(cum)

    src = jnp.concatenate([dt, tailw, ecum], axis=0)
    src_hi = src.astype(bf16)
    src_lo = (src - src_hi.astype(f32)).astype(bf16)
    pieces = jnp.concatenate([src_hi, src_lo], axis=0)

    for g in range(SSM_GROUPS):
        gs = slice(g * hp, (g + 1) * hp)
        ex = jnp.dot(pieces, ex_ref[:, gs], preferred_element_type=f32)
        dt_x = ex[0:L] + ex[3 * L:4 * L]
        tw_x = ex[L:2 * L] + ex[4 * L:5 * L]
        ec_x = ex[2 * L:3 * L] + ex[5 * L:6 * L]
        x_g = xc_scr[:, gs]
        b_g = xc_scr[:, D_INNER + g * D_STATE:D_INNER + (g + 1) * D_STATE].astype(bf16)
        c_g = xc_scr[:, D_INNER + SSM_GROUPS * D_STATE + g * D_STATE:
                     D_INNER + SSM_GROUPS * D_STATE + (g + 1) * D_STATE].astype(bf16)
        cb = _dot_nt(c_g, b_g)
        ht_g = h_scr[g]
        y_in = jnp.dot(c_g, ht_g.astype(bf16), preferred_element_type=f32)
        xdt = (x_g * dt_x).astype(bf16)
        ys = []
        for hh in range(SSM_HEADS_PER_GROUP):
            h = g * SSM_HEADS_PER_GROUP + hh
            seg = cum[:, h:h + 1] - cum_t[h:h + 1, :]
            dec = jnp.exp(jnp.where(causal, seg, NEG_INF))
            m = (cb * dec).astype(bf16)
            ys.append(jnp.dot(m, xdt[:, hh * SSM_HEAD_DIM:(hh + 1) * SSM_HEAD_DIM], preferred_element_type=f32))
        yg = jnp.concatenate(ys, axis=1) + y_in * ec_x + x_g * dsk_ref[:, gs]
        upd = _dot_tn(b_g, (x_g * tw_x).astype(bf16))
        h_scr[g] = ht_g * ec_x[L - 1:L, :] + upd
        yg = yg * _silu(z_ref[:, gs])
        yg = yg * lax.rsqrt(jnp.mean(yg * yg, axis=-1, keepdims=True) + EPS)
        y_ref[:, gs] = (yg * ng_ref[:, gs]).astype(y_ref.dtype)

    @pl.when(c == nc - 1)
    def _():
        for g in range(SSM_GROUPS):
            hout_ref[g * hp:(g + 1) * hp, :] = h_scr[g].T


def ssd_mixer(z, xbc, dtr, h0, cb0, conv_w, conv_b, dt_bias, a_log, d_skip, norm_g, *, t_valid, z_block=0):
    b, t, _ = z.shape
    L = SSD_CHUNK
    assert t % L == 0
    nc = t // L
    hp_all = SSM_HEADS * SSM_HEAD_DIM
    pad_h = LANES - SSM_HEADS

    def lane_pad(v):
        return jnp.pad(v.astype(f32), (0, pad_h)).reshape(1, LANES)

    cw = jnp.pad(conv_w.astype(f32), ((0, SUBLANES - CONV_W), (0, 0)))
    head_of_col = jnp.arange(hp_all) // SSM_HEAD_DIM
    expand = (jnp.arange(LANES)[:, None] == head_of_col[None, :]).astype(bf16)
    dsk_cols = jnp.repeat(d_skip.astype(f32), SSM_HEAD_DIM).reshape(1, hp_all)
    const = lambda shape: pl.BlockSpec(shape, lambda i, j: (0,) * len(shape))
    return pl.pallas_call(
        functools.partial(_ssd_kernel, chunk=L, t_valid=t_valid),
        out_shape=(jax.ShapeDtypeStruct((b, t, D_INNER), bf16),
                   jax.ShapeDtypeStruct((b, hp_all, D_STATE), f32)),
        grid=(b, nc),
        in_specs=[
            pl.BlockSpec((None, L, D_INNER), lambda i, j: (i, j, z_block)),
            pl.BlockSpec((None, L, CONV_DIM), lambda i, j: (i, j, 0)),
            pl.BlockSpec((None, L, LANES), lambda i, j: (i, j, 0)),
            pl.BlockSpec((None, hp_all, D_STATE), lambda i, j: (i, 0, 0)),
            pl.BlockSpec((None, SUBLANES, CONV_DIM), lambda i, j: (i, 0, 0)),
            const((SUBLANES, CONV_DIM)),
            const((1, CONV_DIM)),
            const((1, LANES)), const((1, LANES)), const((1, D_INNER)),
            const((1, D_INNER)),
            const((LANES, hp_all)),
        ],
        out_specs=(pl.BlockSpec((None, L, D_INNER), lambda i, j: (i, j, 0)),
                   pl.BlockSpec((None, hp_all, D_STATE), lambda i, j: (i, 0, 0))),
        scratch_shapes=[pltpu.VMEM((SSM_GROUPS, D_STATE, hp_all // SSM_GROUPS), f32),
                        pltpu.VMEM((2 * SUBLANES, CONV_DIM), f32),
                        pltpu.VMEM((L, CONV_DIM), f32)],
        compiler_params=_cparams(("parallel", "arbitrary")),
        name="ssd_mixer",
    )(z, xbc, dtr, h0, cb0, cw, conv_b.reshape(1, CONV_DIM).astype(f32),
      lane_pad(dt_bias), lane_pad(a_log), dsk_cols, norm_g.reshape(1, D_INNER).astype(f32), expand)


def _sort_network(n):
    def merge(lo, hi, r):
        step = r * 2
        if step < hi - lo:
            yield from merge(lo, hi, step)
            yield from merge(lo + r, hi, step)
            yield from [(i, i + r) for i in range(lo + r, hi - r, step)]
        else:
            yield (lo, lo + r)

    def sort(lo, hi):
        if hi - lo >= 1:
            mid = lo + (hi - lo) // 2
            yield from sort(lo, mid)
            yield from sort(mid + 1, hi)
            yield from merge(lo, hi, 1)

    return list(sort(0, n - 1))


def _sublane_max(x):
    for shift in (4, 2, 1):
        x = jnp.maximum(x, pltpu.roll(x, shift, 0))
    return x


def _pop_columns(cols, extra, n):
    cols = list(cols)
    out = []
    for r in range(n):
        top = cols[0] if extra is None else jnp.maximum(cols[0], extra)
        m = _sublane_max(top)
        out.append(m)
        if r + 1 == n:
            break
        hit = cols[0] == m
        for i in range(min(n - r - 1, len(cols))):
            below = cols[i + 1] if i + 1 < len(cols) else NEG_INF
            cols[i] = jnp.where(hit, below, cols[i])
        if extra is not None:
            extra = jnp.where(extra == m, NEG_INF, extra)
    return out


def _top_values(s, n):
    k = s.shape[0] // SUBLANES
    v = [s[i * SUBLANES:(i + 1) * SUBLANES] for i in range(k)]
    for i, j in _sort_network(k):
        v[i], v[j] = jnp.maximum(v[i], v[j]), jnp.minimum(v[i], v[j])
    return _pop_columns(v[:n], None, n)


def _bf16_pair_words(x):
    hi = pltpu.bitcast(x.astype(bf16).astype(f32), jnp.uint32)
    return hi | (hi >> 16)


def _row_as_bf16_tile(words, rows, zero_words=None):
    tile = jnp.broadcast_to(words, (SUBLANES, words.shape[1]))
    if zero_words is not None:
        tile = tile + zero_words
    packed = pltpu.bitcast(tile, bf16)
    return jnp.concatenate([packed] * (rows // packed.shape[0]), axis=0)


def _peer_scores_kernel(h_ref, g_ref, wq_ref, sk_ref, xnt_ref, rank_ref, n1_ref, c1_ref, e2_ref, q_scr):
    x = h_ref[...]
    ms = jnp.mean(x * x, axis=-1, keepdims=True)
    xn = x * lax.rsqrt(ms + EPS) * g_ref[...]
    xnt_ref[...] = xn.T.astype(bf16)
    q = jnp.dot(xn.astype(bf16), wq_ref[...], preferred_element_type=f32).astype(bf16)
    for i in range(2 * PEER_HEADS):
        q_scr[i] = q[:, i * PEER_HALF:(i + 1) * PEER_HALF]
    kk = PEER_TOPK
    sk0 = sk_ref[0]
    sk1 = sk_ref[1]

    nkeys = sk0.shape[0]
    tokens = h_ref.shape[0]

    def rep(tile):
        return jnp.concatenate([tile] * (nkeys // SUBLANES), axis=0)

    def select_half(s1, s2):
        av = _top_values(s1, kk)
        bv = _top_values(s2, kk)
        sub = lax.broadcasted_iota(jnp.int32, av[0].shape, 0)
        a_lo, a_hi = av[SUBLANES - 1], av[2 * SUBLANES - 1]
        for r in range(SUBLANES - 2, -1, -1):
            a_lo = jnp.where(sub == r, av[r], a_lo)
            a_hi = jnp.where(sub == r, av[SUBLANES + r], a_hi)
        cand = [a_lo + bv[0]]
        for c in range(1, kk):
            cand.append(jnp.where(sub < kk // (c + 1), a_lo + bv[c], NEG_INF))
        best = _pop_columns(cand, a_hi + bv[0], kk)
        tau, top = best[kk - 1], best[0]
        zsum = jnp.zeros_like(top)
        for r in range(kk):
            zsum = zsum + jnp.exp(best[r] - top)
        tau_r = rep(tau)
        rank2 = jnp.full(s2.shape, float(kk), f32)
        n1 = jnp.zeros(s1.shape, f32)
        for c in range(kk - 1, -1, -1):
            b_c = rep(bv[c])
            rank2 = jnp.where(s2 >= b_c, float(c), rank2)
            n1 = n1 + jnp.where(s1 + b_c >= tau_r, 1.0, 0.0)
        c1 = jnp.exp(s1 - rep(av[0])) / rep(zsum)
        e2 = jnp.exp(s2 - rep(bv[0]))
        return rank2, n1, c1, e2

    def head(h, carry):
        s1 = _dot_nt(sk0, q_scr[2 * h])
        s2 = _dot_nt(sk1, q_scr[2 * h + 1])
        parts = [select_half(s1[:, l0:l0 + LANES], s2[:, l0:l0 + LANES]) for l0 in range(0, tokens, LANES)]
        rank2, n1, c1, e2 = (jnp.concatenate(p, axis=1) for p in zip(*parts))
        rank_ref[h] = rank2.astype(rank_ref.dtype)
        n1_ref[h] = _bf16_pair_words(n1)
        c1_ref[h] = _bf16_pair_words(c1)
        e2_ref[h] = e2.astype(e2_ref.dtype)
        return carry

    lax.fori_loop(0, PEER_HEADS, head, 0, unroll=True)


def peer_scores(h, gain, wq, sk, *, tm):
    n, d = h.shape
    nq = wq.shape[1]
    sel16 = jax.ShapeDtypeStruct((PEER_HEADS, PEER_N_KEYS, n), bf16)
    sel32 = jax.ShapeDtypeStruct((PEER_HEADS, PEER_N_KEYS, n), jnp.uint32)
    sel_spec = pl.BlockSpec((PEER_HEADS, PEER_N_KEYS, tm), lambda i: (0, 0, i))
    return pl.pallas_call(
        _peer_scores_kernel,
        out_shape=(jax.ShapeDtypeStruct((d, n), bf16), sel16, sel32, sel32, sel16),
        grid=(n // tm,),
        in_specs=[pl.BlockSpec((tm, d), lambda i: (i, 0)),
                  pl.BlockSpec((1, d), lambda i: (0, 0)),
                  pl.BlockSpec((d, nq), lambda i: (0, 0)),
                  pl.BlockSpec((2, PEER_N_KEYS, PEER_HALF), lambda i: (0, 0, 0))],
        out_specs=(pl.BlockSpec((d, tm), lambda i: (0, i)), sel_spec, sel_spec, sel_spec, sel_spec),
        scratch_shapes=[pltpu.VMEM((2 * PEER_HEADS, tm, PEER_HALF), bf16)],
        compiler_params=_cparams(("parallel",)),
        name="peer_scores",
    )(h, gain.reshape(1, d).astype(f32), wq, sk)


def _peer_dense_kernel(xnt_ref, u_ref, vt_ref, rank_ref, n1_ref, c1_ref, e2_ref, h_ref, o_ref,
                       acc_scr, coef_a, coef_b, *, te, sub, ne):
    j = pl.program_id(1)
    nk = PEER_N_KEYS
    a_per_step = te // nk
    a_per_sub = sub // nk
    tm = xnt_ref.shape[1]
    pk = 2 * SUBLANES
    zero = jnp.zeros((), bf16)

    def drain(src, part, parts):
        d = acc_scr.shape[0]
        rows = slice(part * d // parts, (part + 1) * d // parts)
        upd = jnp.dot(vt_ref[rows, :], src[...], preferred_element_type=f32)
        acc_scr[rows, :] += upd
        bits = pltpu.bitcast(upd[0:SUBLANES, :], jnp.uint32)
        return lax.shift_right_logical(bits, jnp.uint32(32))

    def build(dst, src):
        xnt = xnt_ref[...]
        nsub = te // sub

        def scores(s):
            return jnp.dot(u_ref[s * sub:(s + 1) * sub, :], xnt, preferred_element_type=f32)

        def masks(act, s, token):
            for ai in range(a_per_sub):
                ar = s * a_per_sub + ai
                grp = j * (a_per_step // SUBLANES) + ar // SUBLANES
                row = ar % SUBLANES
                wsum = None
                for h in range(PEER_HEADS):
                    tie = token if (h == 0 and token is not None) else None
                    n1a = _row_as_bf16_tile(n1_ref[h, grp, row:row + 1, :], nk, tie)
                    c1a = _row_as_bf16_tile(c1_ref[h, grp, row:row + 1, :], nk)
                    term = jnp.where(rank_ref[h] < n1a, e2_ref[h], zero) * c1a
                    wsum = term if wsum is None else wsum + term
                g = _gelu_tanh(act[ai * nk:(ai + 1) * nk].astype(bf16))
                dst[ar * nk:(ar + 1) * nk, :] = wsum * g

        token = None
        for s in range(nsub):
            act = scores(s)
            nxt = drain(src, s, nsub) if src is not None else None
            masks(act, s, token)
            token = nxt

    @pl.when(j == 0)
    def _():
        acc_scr[...] = jnp.zeros_like(acc_scr)
        build(coef_a, None)

    @pl.when(jnp.logical_and(jnp.logical_and(j > 0, j < ne), j % 2 == 0))
    def _():
        build(coef_a, coef_b)

    @pl.when(jnp.logical_and(j < ne, j % 2 == 1))
    def _():
        build(coef_b, coef_a)

    @pl.when(j == ne)
    def _():
        drain(coef_b if ne % 2 == 0 else coef_a, 0, 1)
        o_ref[...] = h_ref[...] + acc_scr[...].T


def peer_dense(xnt, u, vt, rank2, n1, c1, e2, h, *, tm, te, sub):
    n, d = h.shape
    n_exp = u.shape[0]
    ne = n_exp // te
    assert (te // PEER_N_KEYS) % SUBLANES == 0
    sel_spec = pl.BlockSpec((PEER_HEADS, PEER_N_KEYS, tm), lambda i, j: (0, 0, i))
    ngrp = PEER_N_KEYS // SUBLANES
    n1 = n1.reshape(PEER_HEADS, ngrp, SUBLANES, n)
    c1 = c1.reshape(PEER_HEADS, ngrp, SUBLANES, n)
    row_spec = pl.BlockSpec((PEER_HEADS, ngrp, SUBLANES, tm), lambda i, j: (0, 0, 0, i))
    return pl.pallas_call(
        functools.partial(_peer_dense_kernel, te=te, sub=sub, ne=ne),
        out_shape=jax.ShapeDtypeStruct((n, d), f32),
        grid=(n // tm, ne + 1),
        in_specs=[pl.BlockSpec((d, tm), lambda i, j: (0, i)),
                  pl.BlockSpec((te, d), lambda i, j: (jnp.minimum(j, ne - 1), 0)),
                  pl.BlockSpec((d, te), lambda i, j: (0, jnp.maximum(j - 1, 0))),
                  sel_spec, row_spec, row_spec, sel_spec,
                  pl.BlockSpec((tm, d), lambda i, j: (i, 0))],
        out_specs=pl.BlockSpec((tm, d), lambda i, j: (i, 0)),
        scratch_shapes=[pltpu.VMEM((d, tm), f32), pltpu.VMEM((te, tm), bf16), pltpu.VMEM((te, tm), bf16)],
        compiler_params=_cparams(("parallel", "arbitrary")),
        name="peer_dense",
    )(xnt, u, vt, rank2, n1, c1, e2, h)


ATTN_BLOCK = 128


def _attn_band_kernel(*refs, dil):
    nq = Q_PER_KV * KV_PER_BRANCH * HEAD_DIM // LANES
    q_refs = refs[:nq]
    kc_ref, kp_ref, vc_ref, vp_ref, o_ref, l_ref, o_scr, l_scr = refs[nq:]
    i = pl.program_id(1)
    blk = ATTN_BLOCK
    scale = HEAD_DIM ** -0.5
    rows = Q_PER_KV * blk
    a_i = lax.broadcasted_iota(jnp.int32, (rows, blk), 0) % blk
    c_i = lax.broadcasted_iota(jnp.int32, (rows, blk), 1)
    mask_c = c_i <= a_i
    mask_p = jnp.logical_and(c_i >= a_i, i > 0)

    def residue(r, carry):
        sel = pl.ds(r, blk, stride=dil) if dil > 1 else pl.ds(0, blk)
        q = jnp.concatenate([qr[sel, :] for qr in q_refs], axis=1) * scale
        kc, kp, vc, vp = kc_ref[sel, :], kp_ref[sel, :], vc_ref[sel, :], vp_ref[sel, :]
        outs, lses = [], []
        for j in range(KV_PER_BRANCH):
            hs = slice(j * HEAD_DIM, (j + 1) * HEAD_DIM)
            qs = jnp.concatenate([q[:, (j * Q_PER_KV + x) * HEAD_DIM:(j * Q_PER_KV + x + 1) * HEAD_DIM]
                                  for x in range(Q_PER_KV)], axis=0).astype(bf16)
            s_c = jnp.where(mask_c, _dot_nt(qs, kc[:, hs].astype(bf16)), NEG_INF)
            s_p = jnp.where(mask_p, _dot_nt(qs, kp[:, hs].astype(bf16)), NEG_INF)
            m = jnp.maximum(jnp.max(s_c, axis=-1, keepdims=True), jnp.max(s_p, axis=-1, keepdims=True))
            p_c = jnp.exp(s_c - m)
            p_p = jnp.exp(s_p - m)
            l = jnp.sum(p_c, axis=-1, keepdims=True) + jnp.sum(p_p, axis=-1, keepdims=True)
            o = jnp.dot(p_c.astype(bf16), vc[:, hs].astype(bf16), preferred_element_type=f32)
            o = o + jnp.dot(p_p.astype(bf16), vp[:, hs].astype(bf16), preferred_element_type=f32)
            o = o / l
            lse = jnp.broadcast_to(m + jnp.log(l), (rows, HEAD_DIM))
            for x in range(Q_PER_KV):
                outs.append(o[x * blk:(x + 1) * blk])
                lses.append(lse[x * blk:(x + 1) * blk])
        per = LANES // HEAD_DIM
        for cb in range(nq):
            o_scr[cb, sel, :] = jnp.concatenate(outs[cb * per:(cb + 1) * per], axis=1)
            l_scr[cb, sel, :] = jnp.concatenate(lses[cb * per:(cb + 1) * per], axis=1)
        return carry

    lax.fori_loop(0, dil, residue, 0)
    for cb in range(nq):
        o_ref[:, cb * LANES:(cb + 1) * LANES] = o_scr[cb]
        l_ref[:, cb * LANES:(cb + 1) * LANES] = l_scr[cb]


def attn_band(q, k, v, g, dil):
    b, t, _ = q.shape
    span = ATTN_BLOCK * dil
    assert t % span == 0
    hq = Q_PER_KV * KV_PER_BRANCH * HEAD_DIM
    hk = KV_PER_BRANCH * HEAD_DIM
    cur = lambda bi, i: (bi, i, g)
    prev = lambda bi, i: (bi, jnp.maximum(i - 1, 0), g)
    osd = jax.ShapeDtypeStruct((b, t, hq), f32)
    nq = hq // LANES
    q_specs = [pl.BlockSpec((None, span, LANES), functools.partial(lambda bi, i, cb: (bi, i, g * nq + cb), cb=cb))
               for cb in range(nq)]
    return pl.pallas_call(
        functools.partial(_attn_band_kernel, dil=dil),
        out_shape=(osd, osd),
        grid=(b, t // span),
        in_specs=q_specs + [pl.BlockSpec((None, span, hk), cur), pl.BlockSpec((None, span, hk), prev),
                            pl.BlockSpec((None, span, hk), cur), pl.BlockSpec((None, span, hk), prev)],
        out_specs=(pl.BlockSpec((None, span, hq), lambda bi, i: (bi, i, 0)),
                   pl.BlockSpec((None, span, hq), lambda bi, i: (bi, i, 0))),
        scratch_shapes=[pltpu.VMEM((nq, span, LANES), f32), pltpu.VMEM((nq, span, LANES), f32)],
        compiler_params=_cparams(("parallel", "arbitrary")),
        name=f"attn_band_d{dil}",
    )(*([q] * nq), k, k, v, v)


def _merge3_kernel(o0, o1, o2, l0, l1, l2, out_ref):
    a0, a1, a2 = l0[...], l1[...], l2[...]
    m = jnp.maximum(jnp.maximum(a0, a1), a2)
    w0, w1, w2 = jnp.exp(a0 - m), jnp.exp(a1 - m), jnp.exp(a2 - m)
    out_ref[...] = ((w0 * o0[...] + w1 * o1[...] + w2 * o2[...]) / (w0 + w1 + w2)).astype(out_ref.dtype)


def merge3(os_, ls_, *, tm):
    n, c = os_[0].shape
    spec = pl.BlockSpec((tm, c), lambda i: (i, 0))
    return pl.pallas_call(
        _merge3_kernel,
        out_shape=jax.ShapeDtypeStruct((n, c), bf16),
        grid=(n // tm,),
        in_specs=[spec] * 6,
        out_specs=spec,
        compiler_params=_cparams(("parallel",)),
        name="attn_merge",
    )(*os_, *ls_)


def _attn_sample_kernel(q_ref, ck_ref, cv_ref, nk_ref, nv_ref, o_ref, *, t_new, win_buf):
    scale = HEAD_DIM ** -0.5
    q = q_ref[...] * scale
    n_cache = ck_ref.shape[0]
    n_new = nk_ref.shape[0]
    rows = Q_PER_KV * t_new
    t_c = lax.broadcasted_iota(jnp.int32, (rows, n_cache), 0) % t_new
    d_c = win_buf + t_c - lax.broadcasted_iota(jnp.int32, (rows, n_cache), 1)
    t_n = lax.broadcasted_iota(jnp.int32, (rows, n_new), 0) % t_new
    d_n = t_n - lax.broadcasted_iota(jnp.int32, (rows, n_new), 1)
    outs = [None] * (KV_PER_BRANCH * Q_PER_KV)
    for j in range(KV_PER_BRANCH):
        parts = []
        for g, (win, dil) in enumerate(DIL_PATTERNS):
            hq = (g * KV_PER_BRANCH + j) * Q_PER_KV
            hk = slice((g * KV_PER_BRANCH + j) * HEAD_DIM, (g * KV_PER_BRANCH + j + 1) * HEAD_DIM)
            qs = jnp.concatenate([q[:, (hq + r) * HEAD_DIM:(hq + r + 1) * HEAD_DIM] for r in range(Q_PER_KV)],
                                 axis=0).astype(bf16)
            ok_c = (d_c >= 0) & (d_c <= win) & ((d_c & (dil - 1)) == 0)
            ok_n = (d_n >= 0) & (d_n <= win) & ((d_n & (dil - 1)) == 0)
            s_c = jnp.where(ok_c, _dot_nt(qs, ck_ref[:, hk].astype(bf16)), NEG_INF)
            s_n = jnp.where(ok_n, _dot_nt(qs, nk_ref[:, hk].astype(bf16)), NEG_INF)
            parts.append((s_c, s_n, hk))
        m = None
        for s_c, s_n, _ in parts:
            mm = jnp.maximum(jnp.max(s_c, axis=-1, keepdims=True), jnp.max(s_n, axis=-1, keepdims=True))
            m = mm if m is None else jnp.maximum(m, mm)
        l = jnp.zeros_like(m)
        o = jnp.zeros((rows, HEAD_DIM), f32)
        for s_c, s_n, hk in parts:
            p_c = jnp.exp(s_c - m)
            p_n = jnp.exp(s_n - m)
            l = l + jnp.sum(p_c, axis=-1, keepdims=True) + jnp.sum(p_n, axis=-1, keepdims=True)
            o = o + jnp.dot(p_c.astype(bf16), cv_ref[:, hk].astype(bf16), preferred_element_type=f32)
            o = o + jnp.dot(p_n.astype(bf16), nv_ref[:, hk].astype(bf16), preferred_element_type=f32)
        o = o / l
        for r in range(Q_PER_KV):
            outs[j * Q_PER_KV + r] = o[r * t_new:(r + 1) * t_new]
    o_ref[...] = jnp.concatenate(outs, axis=1).astype(o_ref.dtype)


def attn_sample(q, ck, cv, nk, nv):
    b, t_new, qc = q.shape
    w = ck.shape[1]
    kc = ck.shape[2]
    n_new = nk.shape[1]
    return pl.pallas_call(
        functools.partial(_attn_sample_kernel, t_new=t_new, win_buf=w),
        out_shape=jax.ShapeDtypeStruct((b, t_new, ATTN_OUT_DIM), bf16),
        grid=(b,),
        in_specs=[pl.BlockSpec((None, t_new, qc), lambda i: (i, 0, 0)),
                  pl.BlockSpec((None, w, kc), lambda i: (i, 0, 0)),
                  pl.BlockSpec((None, w, kc), lambda i: (i, 0, 0)),
                  pl.BlockSpec((None, n_new, kc), lambda i: (i, 0, 0)),
                  pl.BlockSpec((None, n_new, kc), lambda i: (i, 0, 0))],
        out_specs=pl.BlockSpec((None, t_new, ATTN_OUT_DIM), lambda i: (i, 0, 0)),
        compiler_params=_cparams(("parallel",)),
        name="attn_sample",
    )(q, ck, cv, nk, nv)


def _prep_weights(norm_mix, norm_ffn, norm_ple, ssm_w_in, ssm_conv_w, ssm_conv_b, ssm_dt_bias, ssm_a_log, ssm_d,
                  ssm_norm, ssm_w_out, kv_norm, w_kv, k_norm, w_q, q_norm, w_o,
                  peer_w_query, peer_sub_keys, peer_u, peer_v, ple_w_proj, ple_w_gate):
    zx = D_INNER + CONV_DIM
    w_in = ssm_w_in[0]
    kdim = N_KV_HEADS * HEAD_DIM
    rep = LANES // HEAD_DIM
    return dict(
        norm_mix=norm_mix, norm_ffn=norm_ffn, norm_ple=norm_ple,
        w_z=w_in[:, :D_INNER].astype(bf16), w_xbc=w_in[:, D_INNER:zx].astype(bf16),
        w_xz=jnp.concatenate([w_in[:, D_INNER:zx], w_in[:, :D_INNER]], axis=1).astype(bf16),
        w_dt=jnp.pad(w_in[:, zx:], ((0, 0), (0, LANES - SSM_HEADS))).astype(bf16),
        conv_w=ssm_conv_w[0], conv_b=ssm_conv_b[0], dt_bias=ssm_dt_bias[0], a_log=ssm_a_log[0],
        d_skip=ssm_d[0], ssm_norm=ssm_norm[0], w_out=ssm_w_out[0].astype(bf16),
        kv_norm=kv_norm, w_k=w_kv[:, :kdim].astype(bf16), w_v=w_kv[:, kdim:].astype(bf16),
        k_gain=jnp.tile(k_norm.astype(f32), rep).reshape(1, LANES),
        w_q=w_q[0].astype(bf16), q_gain=jnp.tile(q_norm[0].astype(f32), rep).reshape(1, LANES),
        w_o=w_o[0].astype(bf16),
        peer_wq=[peer_w_query[i].astype(bf16) for i in range(DEPTH)],
        peer_sk=[peer_sub_keys[i].astype(bf16) for i in range(DEPTH)],
        peer_u=[peer_u[i].astype(bf16) for i in range(DEPTH)],
        peer_vt=[peer_v[i].astype(bf16).T for i in range(DEPTH)],
        ple_wp=[ple_w_proj[i].astype(bf16) for i in range(DEPTH)],
        ple_wg=[ple_w_gate[i].astype(bf16) for i in range(DEPTH)],
    )


PEER_TOKEN_TILE = 512
PEER_SCORE_TILE = 256
PEER_EXPERT_TILE = 2 * SUBLANES * PEER_N_KEYS
PEER_SUB_TILE = 1024


def _peer_ple(h, p, i, wts, tm):
    n = h.shape[0]
    tm_d = PEER_TOKEN_TILE if n % PEER_TOKEN_TILE == 0 else n
    xnt, rank2, n1, c1, e2 = peer_scores(h, wts["norm_ffn"][i], wts["peer_wq"][i], wts["peer_sk"][i],
                                         tm=PEER_SCORE_TILE)
    h = peer_dense(xnt, wts["peer_u"][i], wts["peer_vt"][i], rank2, n1, c1, e2, h,
                   tm=tm_d, te=PEER_EXPERT_TILE, sub=PEER_SUB_TILE)
    return fused_matmul(h, wts["ple_wg"][i], gain=wts["norm_ple"][i], mode="ple", res=h,
                        ple=(p, wts["ple_wp"][i]), tm=tm)


def _trunk(x, p, conv_state, ssm_state, past_k, past_v, pos0, wts):
    b, t, d = x.shape
    n = b * t
    prompt = past_k is None
    tm = _token_tile(n)
    h = x.reshape(n, d)

    xz = fused_matmul(h, wts["w_xz"], gain=wts["norm_mix"][0], tm=tm).reshape(b, t, CONV_DIM + D_INNER)
    dtr = fused_matmul(h, wts["w_dt"], gain=wts["norm_mix"][0], tm=tm)
    new_conv = xz[:, t - (CONV_W - 1):, :CONV_DIM] if t >= CONV_W - 1 else None
    dtr = dtr.reshape(b, t, LANES)
    tp = -(-t // SSD_CHUNK) * SSD_CHUNK
    if tp != t:
        padt = ((0, 0), (0, tp - t), (0, 0))
        xz_p, dtr_p = jnp.pad(xz, padt), jnp.pad(dtr, padt)
    else:
        xz_p, dtr_p = xz, dtr
    cb0 = jnp.pad(conv_state.astype(f32), ((0, 0), (SUBLANES - (CONV_W - 1), 0), (0, 0)))
    h0 = ssm_state.astype(f32).reshape(b, SSM_HEADS * SSM_HEAD_DIM, D_STATE)
    y, h_fin = ssd_mixer(xz_p, xz_p, dtr_p, h0, cb0, wts["conv_w"], wts["conv_b"], wts["dt_bias"], wts["a_log"],
                         wts["d_skip"], wts["ssm_norm"], t_valid=(None if tp == t else t),
                         z_block=CONV_DIM // D_INNER)
    y = y[:, :t].reshape(n, D_INNER)
    h = fused_matmul(y, wts["w_out"], mode="add", res=h, tm=tm)
    new_ssm = h_fin.reshape(b, SSM_HEADS, SSM_HEAD_DIM, D_STATE)
    h = _peer_ple(h, p[0].reshape(n, -1), 0, wts, tm)

    pos = pos0 + jnp.arange(t, dtype=jnp.int32)
    tabs = _rope_tables(pos)
    if t < tm:
        tabs = tuple(jnp.tile(a, (tm // t, 1)) for a in tabs)
    k_new = fused_matmul(h, wts["w_k"], gain=wts["kv_norm"], mode="headrope",
                         headrope=(wts["k_gain"],) + tabs, tm=tm)
    v_new = fused_matmul(h, wts["w_v"], gain=wts["kv_norm"], tm=tm)

    q = fused_matmul(h, wts["w_q"], gain=wts["norm_mix"][1], mode="headrope",
                     headrope=(wts["q_gain"],) + tabs, tm=tm)
    kdim = N_KV_HEADS * HEAD_DIM
    if prompt:
        q3, k3, v3 = q.reshape(b, t, -1), k_new.reshape(b, t, kdim), v_new.reshape(b, t, kdim)
        os_, ls_ = [], []
        for g, (_, dil) in enumerate(DIL_PATTERNS):
            o_g, l_g = attn_band(q3, k3, v3, g, dil)
            os_.append(o_g.reshape(n, ATTN_OUT_DIM))
            ls_.append(l_g.reshape(n, ATTN_OUT_DIM))
        att = merge3(os_, ls_, tm=tm)
    else:
        padn = ((0, 0), (0, SSD_CHUNK - t), (0, 0))
        nk = jnp.pad(k_new.reshape(b, t, kdim), padn)
        nv = jnp.pad(v_new.reshape(b, t, kdim), padn)
        w = past_k.shape[1]
        att = attn_sample(q.reshape(b, t, -1), past_k.reshape(b, w, kdim).astype(f32),
                          past_v.reshape(b, w, kdim).astype(f32), nk, nv).reshape(n, ATTN_OUT_DIM)
    h = fused_matmul(att, wts["w_o"], mode="add", res=h, tm=tm)
    h = _peer_ple(h, p[1].reshape(n, -1), 1, wts, tm)

    return (h.reshape(b, t, d), new_conv[None], new_ssm[None],
            k_new.reshape(b, t, N_KV_HEADS, HEAD_DIM), v_new.reshape(b, t, N_KV_HEADS, HEAD_DIM))


def kernel(x_prompt, x_sample, state_conv, state_ssm, cache_k, cache_v, p_prompt, p_sample, norm_mix, norm_ffn, norm_ple, ssm_w_in, ssm_conv_w, ssm_conv_b, ssm_dt_bias, ssm_a_log, ssm_d, ssm_norm, ssm_w_out, kv_norm, w_kv, k_norm, w_q, q_norm, w_o, peer_w_query, peer_sub_keys, peer_u, peer_v, ple_w_proj, ple_w_gate):
    wts = _prep_weights(norm_mix, norm_ffn, norm_ple, ssm_w_in, ssm_conv_w, ssm_conv_b, ssm_dt_bias, ssm_a_log,
                        ssm_d, ssm_norm, ssm_w_out, kv_norm, w_kv, k_norm, w_q, q_norm, w_o,
                        peer_w_query, peer_sub_keys, peer_u, peer_v, ple_w_proj, ple_w_gate)
    b_p, t_p, _ = x_prompt.shape
    zero_conv = jnp.zeros((b_p, CONV_W - 1, CONV_DIM), x_prompt.dtype)
    zero_ssm = jnp.zeros((b_p, SSM_HEADS, SSM_HEAD_DIM, D_STATE), x_prompt.dtype)
    y_p, conv_p, ssm_p, k_p, v_p = _trunk(x_prompt, p_prompt, zero_conv, zero_ssm, None, None, 0, wts)
    keep = min(max(w for w, _ in DIL_PATTERNS), t_p)
    past_len = 16384
    y_s, conv_s, ssm_s, k_s, v_s = _trunk(x_sample, p_sample, state_conv[0], state_ssm[0], cache_k, cache_v,
                                          past_len, wts)
    if keep < t_p:
        k_p, v_p = k_p[:, -keep:], v_p[:, -keep:]
    return (y_p, y_s, conv_p, ssm_p, k_p, v_p, conv_s, ssm_s, k_s, v_s)
```
